```python
import jax, jax.numpy as jnp
from jax import lax
import numpy as np

D_MODEL = 2048
BATCH = 8
SEQ = 2048
DEPTH = 1

D_MIX = D_MODEL
D_LRU = D_MIX // 2
D_SGU = D_MIX - D_LRU
LRU_HEADS = 8
LRU_BLOCK = D_LRU // LRU_HEADS
CONV_WIDTH = 4
LRU_C = 8.0
SGU_GROUPS = 8
SGU_GROUP_DIM = D_SGU // SGU_GROUPS
CHUNK = 128
PEER_HEADS = 8
N_KEYS = 128
N_EXPERTS = N_KEYS * N_KEYS
PEER_TOPK = 16
D_KEY = 256
PEER_BLOCK = 128
EPS = 1e-6

kernel_name = "hybrid_rglru_sgu_peer_adaln_block"


def rms_norm(x, g):
    xf = x.astype(jnp.float32)
    y = xf * lax.rsqrt(jnp.mean(xf * xf, axis=-1, keepdims=True) + EPS)
    return (y * g.astype(jnp.float32)).astype(x.dtype)


def modulate(h, shift, scale):
    return h * (1 + scale[:, None, :]) + shift[:, None, :]


def causal_depthwise_conv(x, w, b):
    out = lax.conv_general_dilated(
        x, w[:, None, :].astype(x.dtype), window_strides=(1,),
        padding=[(CONV_WIDTH - 1, 0)],
        dimension_numbers=("NWC", "WIO", "NWC"),
        feature_group_count=x.shape[-1])
    return out + b


def rg_lru(x, w_a, b_a, w_i, b_i, lam):
    B, S, _ = x.shape
    xh = x.reshape(B, S, LRU_HEADS, LRU_BLOCK)
    r = jax.nn.sigmoid(jnp.einsum('bshi,hij->bshj', xh, w_a) + b_a).reshape(B, S, D_LRU)
    i = jax.nn.sigmoid(jnp.einsum('bshi,hij->bshj', xh, w_i) + b_i).reshape(B, S, D_LRU)
    log_a = -LRU_C * r.astype(jnp.float32) * jax.nn.softplus(-lam.astype(jnp.float32))
    a = jnp.exp(log_a)
    b = jnp.sqrt(-jnp.expm1(2.0 * log_a)) * (i * x).astype(jnp.float32)

    def combine(left, right):
        a1, b1 = left
        a2, b2 = right
        return a1 * a2, a2 * b1 + b2

    _, h = lax.associative_scan(combine, (a, b), axis=1)
    return h.astype(x.dtype)


def spatial_gating(u, v, g_v, w_s, b_s):
    B, S, _ = v.shape
    nc = S // CHUNK
    v = rms_norm(v, g_v)
    vv = v.reshape(B, nc, CHUNK, SGU_GROUPS, SGU_GROUP_DIM)
    mask = jnp.tril(jnp.ones((CHUNK, CHUNK), dtype=bool))
    w = jnp.where(mask[None], w_s, 0)
    s = jnp.einsum('gij,bnjgd->bnigd', w, vv) + b_s.T[None, None, :, :, None]
    return u * s.reshape(B, S, D_SGU)


def hybrid_mixer(h, w_in, conv_w, conv_b, w_gate_a, b_gate_a, w_gate_i, b_gate_i,
                 lru_lambda, g_v, w_spatial, b_spatial, w_out):
    proj = h @ w_in
    x_lru, y_gate, u, v = jnp.split(proj, [D_LRU, 2 * D_LRU, 2 * D_LRU + D_SGU], axis=-1)
    xc = causal_depthwise_conv(x_lru, conv_w, conv_b)
    y_rec = rg_lru(xc, w_gate_a, b_gate_a, w_gate_i, b_gate_i, lru_lambda) * jax.nn.gelu(y_gate)
    y_sgu = spatial_gating(jax.nn.gelu(u), jax.nn.gelu(v), g_v, w_spatial, b_spatial)
    return jnp.concatenate([y_rec, y_sgu], axis=-1) @ w_out


def peer(h, w_query, sub_keys, expert_u, expert_v):
    B, S, D = h.shape
    q = (h @ w_query).reshape(B, S, PEER_HEADS, 2, D_KEY // 2)
    s = jnp.einsum('bshpk,hpnk->bshpn', q, sub_keys).astype(jnp.float32)
    v1, i1 = lax.top_k(s[..., 0, :], PEER_TOPK)
    v2, i2 = lax.top_k(s[..., 1, :], PEER_TOPK)
    n_cand = PEER_TOPK * PEER_TOPK
    cand = (v1[..., :, None] + v2[..., None, :]).reshape(B, S, PEER_HEADS, n_cand)
    cand_idx = (i1[..., :, None] * N_KEYS + i2[..., None, :]).reshape(B, S, PEER_HEADS, n_cand)
    top_v, pos = lax.top_k(cand, PEER_TOPK)
    idx = jnp.take_along_axis(cand_idx, pos, axis=-1)
    g = jax.nn.softmax(top_v, axis=-1)
    n_sel = PEER_HEADS * PEER_TOPK
    nb = (B * S) // PEER_BLOCK
    xb = h.reshape(nb, PEER_BLOCK, D)
    ib = idx.reshape(nb, PEER_BLOCK, n_sel)
    gb = g.astype(h.dtype).reshape(nb, PEER_BLOCK, n_sel)

    def block(args):
        xt, it, gt = args
        u = jnp.take(expert_u, it, axis=0)
        act = jax.nn.gelu(jnp.einsum('td,ted->te', xt, u)) * gt
        vv = jnp.take(expert_v, it, axis=0)
        return jnp.einsum('te,ted->td', act, vv)

    out = lax.map(block, (xb, ib, gb))
    return out.reshape(B, S, D)


def setup_inputs(seed: int = 0) -> dict:
    key = jax.random.key(seed)
    k = jax.random.split(key, 24)
    f32 = jnp.float32

    def nrm(kk, shape, scale):
        return jax.random.normal(kk, shape, f32) * scale

    L = DEPTH
    a0 = jax.random.uniform(k[10], (L, D_LRU), f32, minval=0.9, maxval=0.999)
    return {
        "x": nrm(k[0], (BATCH, SEQ, D_MODEL), 1.0),
        "c": nrm(k[1], (BATCH, D_MODEL), 1.0),
        "w_ada": nrm(k[2], (L, D_MODEL, 6 * D_MODEL), 0.5 * D_MODEL ** -0.5),
        "b_ada": nrm(k[3], (L, 6 * D_MODEL), 0.01),
        "g_norm_mix": 1.0 + nrm(k[4], (L, D_MODEL), 0.02),
        "w_in": nrm(k[5], (L, D_MODEL, 2 * D_LRU + 2 * D_SGU), D_MODEL ** -0.5),
        "conv_w": nrm(k[6], (L, CONV_WIDTH, D_LRU), CONV_WIDTH ** -0.5),
        "conv_b": nrm(k[7], (L, D_LRU), 0.01),
        "w_gate_a": nrm(k[8], (L, LRU_HEADS, LRU_BLOCK, LRU_BLOCK), LRU_BLOCK ** -0.5),
        "b_gate_a": nrm(k[9], (L, LRU_HEADS, LRU_BLOCK), 0.01),
        "w_gate_i": nrm(k[11], (L, LRU_HEADS, LRU_BLOCK, LRU_BLOCK), LRU_BLOCK ** -0.5),
        "b_gate_i": nrm(k[12], (L, LRU_HEADS, LRU_BLOCK), 0.01),
        "lru_lambda": jnp.log(a0) - jnp.log1p(-a0),
        "g_v": 1.0 + nrm(k[13], (L, D_SGU), 0.02),
        "w_spatial": nrm(k[14], (L, SGU_GROUPS, CHUNK, CHUNK), 0.5 * CHUNK ** -0.5),
        "b_spatial": 1.0 + nrm(k[15], (L, SGU_GROUPS, CHUNK), 0.01),
        "w_out": nrm(k[16], (L, D_MIX, D_MODEL), D_MIX ** -0.5),
        "g_norm_ffn": 1.0 + nrm(k[17], (L, D_MODEL), 0.02),
        "w_query": nrm(k[18], (L, D_MODEL, PEER_HEADS * D_KEY), D_MODEL ** -0.5),
        "sub_keys": nrm(k[19], (L, PEER_HEADS, 2, N_KEYS, D_KEY // 2), (D_KEY // 2) ** -0.5),
        "expert_u": nrm(k[20], (L, N_EXPERTS, D_MODEL), D_MODEL ** -0.5),
        "expert_v": nrm(k[21], (L, N_EXPERTS, D_MODEL), 1.0),
        "g_final": 1.0 + nrm(k[22], (D_MODEL,), 0.02),
    }


def reference(x, c, w_ada, b_ada, g_norm_mix, w_in, conv_w, conv_b, w_gate_a, b_gate_a,
              w_gate_i, b_gate_i, lru_lambda, g_v, w_spatial, b_spatial, w_out,
              g_norm_ffn, w_query, sub_keys, expert_u, expert_v, g_final):
    for l in range(DEPTH):
        mod = jax.nn.silu(c) @ w_ada[l] + b_ada[l]
        sh_m, sc_m, gt_m, sh_f, sc_f, gt_f = jnp.split(mod, 6, axis=-1)
        h = modulate(rms_norm(x, g_norm_mix[l]), sh_m, sc_m)
        mix = hybrid_mixer(h, w_in[l], conv_w[l], conv_b[l], w_gate_a[l], b_gate_a[l],
                           w_gate_i[l], b_gate_i[l], lru_lambda[l], g_v[l],
                           w_spatial[l], b_spatial[l], w_out[l])
        x = x + gt_m[:, None, :] * mix
        h = modulate(rms_norm(x, g_norm_ffn[l]), sh_f, sc_f)
        x = x + gt_f[:, None, :] * peer(h, w_query[l], sub_keys[l], expert_u[l], expert_v[l])
    return rms_norm(x, g_final)
```

```python
import functools

import jax
import jax.numpy as jnp
from jax import lax
from jax.experimental import pallas as pl
from jax.experimental.pallas import tpu as pltpu

_F32 = jnp.float32
_BF16 = jnp.bfloat16
_I32 = jnp.int32

EPS = 1e-6
LRU_C = 8.0
CONV_WIDTH = 4
PEER_TOPK = 16
LANES = 128
SUBLANES = 8
_NEG_INF = float("-inf")


def _gelu(x):
    return 0.5 * x * (1.0 + jnp.tanh(0.7978845608028654 * (x + 0.044715 * (x * x * x))))


def _softplus(y):
    return jnp.maximum(y, 0.0) + jnp.log1p(jnp.exp(-jnp.abs(y)))


def _rms(x, g):
    return x * lax.rsqrt(jnp.mean(x * x, axis=-1, keepdims=True) + EPS) * g


def _bdot(a, b):
    return jnp.dot(a, b, preferred_element_type=_F32)


def _ada_kernel(c_ref, w_ref, b_ref, o_ref):
    c = c_ref[...]
    a = c * jax.nn.sigmoid(c)
    o_ref[...] = _bdot(a.astype(_BF16), w_ref[...].astype(_BF16)) + b_ref[...]


def _ada_call(c, w, b, tn):
    bsz, d = c.shape
    n = w.shape[1]
    return pl.pallas_call(
        _ada_kernel,
        grid=(n // tn,),
        in_specs=[
            pl.BlockSpec((bsz, d), lambda j: (0, 0)),
            pl.BlockSpec((d, tn), lambda j: (0, j)),
            pl.BlockSpec((1, tn), lambda j: (0, j)),
        ],
        out_specs=pl.BlockSpec((bsz, tn), lambda j: (0, j)),
        out_shape=jax.ShapeDtypeStruct((bsz, n), _F32),
        compiler_params=pltpu.CompilerParams(
            dimension_semantics=("arbitrary",), vmem_limit_bytes=40 * 1024 * 1024),
    )(c, w, b.reshape(1, n))


def _mixer_kernel(x_ref, mod_ref, gmix_ref, win_ref, cw_ref, cb_ref, wa_ref, ba_ref, wi_ref, bi_ref,
                  lam_ref, gv_ref, wsp_ref, bspt_ref, o_ref, ext_ref, hprev_ref, *, ts, d_lru, heads, groups, chunk):
    s = pl.program_id(1)

    @pl.when(s == 0)
    def _():
        ext_ref[0:SUBLANES, :] = jnp.zeros((SUBLANES, d_lru), _F32)
        hprev_ref[...] = jnp.zeros_like(hprev_ref)

    x = x_ref[...]
    shift = mod_ref[0:1, :]
    scale = mod_ref[1:2, :]
    h = _rms(x, gmix_ref[...]) * (1.0 + scale) + shift
    hb = h.astype(_BF16)

    x_lru = _bdot(hb, win_ref[:, 0:d_lru])
    ext_ref[SUBLANES:SUBLANES + ts, :] = x_lru
    xc = cb_ref[...] + cw_ref[3:4, :] * x_lru
    for k in range(CONV_WIDTH - 1):
        off = SUBLANES - (CONV_WIDTH - 1) + k
        xc = xc + cw_ref[k:k + 1, :] * ext_ref[off:off + ts, :]
    ext_ref[0:SUBLANES, :] = x_lru[ts - SUBLANES:ts, :]

    xcb = xc.astype(_BF16)
    blk = d_lru // heads
    ra, ia = [], []
    for hh in range(heads):
        xh = xcb[:, hh * blk:(hh + 1) * blk]
        ra.append(_bdot(xh, wa_ref[hh]))
        ia.append(_bdot(xh, wi_ref[hh]))
    r = jax.nn.sigmoid(jnp.concatenate(ra, axis=1) + ba_ref[...])
    ig = jax.nn.sigmoid(jnp.concatenate(ia, axis=1) + bi_ref[...])
    log_a = (-LRU_C) * r * _softplus(-lam_ref[...])
    a = jnp.exp(log_a)
    bv = jnp.sqrt(-jnp.tanh(log_a) * (a * a + 1.0)) * (ig * xc)

    row = lax.broadcasted_iota(_I32, (ts, d_lru), 0)
    dist = 1
    while dist < ts:
        a_s = pltpu.roll(a, dist, 0)
        b_s = pltpu.roll(bv, dist, 0)
        m = row >= dist
        bv = jnp.where(m, a * b_s + bv, bv)
        a = jnp.where(m, a * a_s, a)
        dist *= 2
    hs = bv + a * hprev_ref[0:1, :]
    hprev_ref[0:1, :] = hs[ts - 1:ts, :]

    y_gate = _bdot(hb, win_ref[:, d_lru:2 * d_lru])
    o_ref[:, 0:d_lru] = (hs * _gelu(y_gate)).astype(_BF16)

    d_sgu = gv_ref.shape[1]
    gd = d_sgu // groups
    u = _gelu(_bdot(hb, win_ref[:, 2 * d_lru:2 * d_lru + d_sgu]))
    v = _gelu(_bdot(hb, win_ref[:, 2 * d_lru + d_sgu:2 * d_lru + 2 * d_sgu]))
    vnb = _rms(v, gv_ref[...]).astype(_BF16)
    tri = lax.broadcasted_iota(_I32, (chunk, chunk), 0) >= lax.broadcasted_iota(_I32, (chunk, chunk), 1)
    for g in range(groups):
        wm = jnp.where(tri, wsp_ref[g], 0.0).astype(_BF16)
        bcol = bspt_ref[:, g:g + 1]
        for n in range(ts // chunk):
            sg = _bdot(wm, vnb[n * chunk:(n + 1) * chunk, g * gd:(g + 1) * gd]) + bcol
            ug = u[n * chunk:(n + 1) * chunk, g * gd:(g + 1) * gd]
            o_ref[n * chunk:(n + 1) * chunk, d_lru + g * gd:d_lru + (g + 1) * gd] = (ug * sg).astype(_BF16)


def _mixer_call(x, mod3, g_mix, w_in_b, conv_w, conv_b, w_a_b, b_a, w_i_b, b_i, lam, g_v, w_sp, b_sp_t, ts):
    bsz, seq, d = x.shape
    d_lru = conv_w.shape[1]
    d_sgu = g_v.shape[1]
    heads = w_a_b.shape[0]
    groups, chunk, _ = w_sp.shape
    d_mix = d_lru + d_sgu
    const = lambda shape: pl.BlockSpec(shape, lambda b, s: (0,) * len(shape), pipeline_mode=pl.Buffered(1))
    kern = functools.partial(_mixer_kernel, ts=ts, d_lru=d_lru, heads=heads, groups=groups, chunk=chunk)
    return pl.pallas_call(
        kern,
        grid=(bsz, seq // ts),
        in_specs=[
            pl.BlockSpec((None, ts, d), lambda b, s: (b, s, 0)),
            pl.BlockSpec((None, 6, d), lambda b, s: (b, 0, 0)),
            const((1, d)),
            const(w_in_b.shape),
            const(conv_w.shape),
            const((1, d_lru)),
            const(w_a_b.shape),
            const((1, d_lru)),
            const(w_i_b.shape),
            const((1, d_lru)),
            const((1, d_lru)),
            const((1, d_sgu)),
            const(w_sp.shape),
            const(b_sp_t.shape),
        ],
        out_specs=pl.BlockSpec((None, ts, d_mix), lambda b, s: (b, s, 0)),
        out_shape=jax.ShapeDtypeStruct((bsz, seq, d_mix), _BF16),
        scratch_shapes=[pltpu.VMEM((ts + SUBLANES, d_lru), _F32), pltpu.VMEM((SUBLANES, d_lru), _F32)],
        compiler_params=pltpu.CompilerParams(
            dimension_semantics=("arbitrary", "arbitrary"), vmem_limit_bytes=56 * 1024 * 1024),
    )(x, mod3, g_mix, w_in_b, conv_w, conv_b, w_a_b, b_a, w_i_b, b_i, lam, g_v, w_sp, b_sp_t)


def _topk_cols(vals, ids, k):
    n, t = vals.shape
    pos = lax.broadcasted_iota(_I32, (n, t), 0)
    kio = lax.broadcasted_iota(_I32, (k, t), 0)
    out_v = jnp.zeros((k, t), _F32)
    out_i = jnp.zeros((k, t), _I32)
    for j in range(k):
        m = jnp.max(vals, axis=0, keepdims=True)
        p = jnp.min(jnp.where(vals == m, pos, n), axis=0, keepdims=True)
        hit = pos == p
        if ids is None:
            e = p
        else:
            e = jnp.max(jnp.where(hit, ids, -1), axis=0, keepdims=True)
        out_v = jnp.where(kio == j, m, out_v)
        out_i = jnp.where(kio == j, e, out_i)
        vals = jnp.where(hit, _NEG_INF, vals)
    return out_v, out_i


def _route_kernel(y_ref, x_ref, mod_ref, wout_ref, gffn_ref, wq_ref, keys_ref,
                  x1_ref, h2_ref, idx_ref, g_ref, *, heads, n_keys):
    x1 = x_ref[...] + mod_ref[2:3, :] * _bdot(y_ref[...], wout_ref[...])
    x1_ref[...] = x1
    h2 = _rms(x1, gffn_ref[...]) * (1.0 + mod_ref[4:5, :]) + mod_ref[3:4, :]
    h2_ref[...] = h2
    q = _bdot(h2.astype(_BF16), wq_ref[...]).astype(_BF16)
    dk = keys_ref.shape[-1]
    nt = (((1,), (1,)), ((), ()))
    for hh in range(heads):
        tops = []
        for p in range(2):
            col = (hh * 2 + p) * dk
            st = lax.dot_general(keys_ref[hh, p], q[:, col:col + dk], nt, preferred_element_type=_F32)
            tops.append(_topk_cols(st, None, PEER_TOPK))
        (v1, i1), (v2, i2) = tops
        cand = jnp.concatenate([v1[i:i + 1, :] + v2 for i in range(PEER_TOPK)], axis=0)
        cidx = jnp.concatenate([i1[i:i + 1, :] * n_keys + i2 for i in range(PEER_TOPK)], axis=0)
        tv, ti = _topk_cols(cand, cidx, PEER_TOPK)
        ex = jnp.exp(tv - tv[0:1, :])
        gsm = ex / jnp.sum(ex, axis=0, keepdims=True)
        idx_ref[hh * PEER_TOPK:(hh + 1) * PEER_TOPK, :] = ti
        g_ref[hh * PEER_TOPK:(hh + 1) * PEER_TOPK, :] = gsm


def _route_call(y, x, mod3, w_out_b, g_ffn, w_q_b, keys_b, tq):
    bsz, seq, d = x.shape
    d_mix = y.shape[-1]
    heads, _, n_keys, dk = keys_b.shape
    nsel = heads * PEER_TOPK
    nq = seq // tq
    const = lambda shape: pl.BlockSpec(shape, lambda b, s: (0,) * len(shape), pipeline_mode=pl.Buffered(1))
    kern = functools.partial(_route_kernel, heads=heads, n_keys=n_keys)
    return pl.pallas_call(
        kern,
        grid=(bsz, nq),
        in_specs=[
            pl.BlockSpec((None, tq, d_mix), lambda b, s: (b, s, 0)),
            pl.BlockSpec((None, tq, d), lambda b, s: (b, s, 0)),
            pl.BlockSpec((None, 6, d), lambda b, s: (b, 0, 0)),
            const(w_out_b.shape),
            const((1, d)),
            const(w_q_b.shape),
            const(keys_b.shape),
        ],
        out_specs=[
            pl.BlockSpec((None, tq, d), lambda b, s: (b, s, 0)),
            pl.BlockSpec((None, tq, d), lambda b, s: (b, s, 0)),
            pl.BlockSpec((nsel, tq), lambda b, s: (0, b * nq + s)),
            pl.BlockSpec((nsel, tq), lambda b, s: (0, b * nq + s)),
        ],
        out_shape=[
            jax.ShapeDtypeStruct((bsz, seq, d), _F32),
            jax.ShapeDtypeStruct((bsz, seq, d), _F32),
            jax.ShapeDtypeStruct((nsel, bsz * seq), _I32),
            jax.ShapeDtypeStruct((nsel, bsz * seq), _F32),
        ],
        compiler_params=pltpu.CompilerParams(
            dimension_semantics=("arbitrary", "arbitrary"), vmem_limit_bytes=56 * 1024 * 1024),
    )(y, x, mod3, w_out_b, g_ffn, w_q_b, keys_b)


def _pack_kernel(u_ref, v_ref, o_ref):
    o_ref[...] = pltpu.pack_elementwise([u_ref[...], v_ref[...]], packed_dtype=_BF16)


def _pack_call(expert_u, expert_v, rows):
    n, d = expert_u.shape
    spec = pl.BlockSpec((rows, d), lambda i: (i, 0))
    return pl.pallas_call(
        _pack_kernel,
        grid=(n // rows,),
        in_specs=[spec, spec],
        out_specs=spec,
        out_shape=jax.ShapeDtypeStruct((n, d), jnp.uint32),
        compiler_params=pltpu.CompilerParams(
            dimension_semantics=("arbitrary",), vmem_limit_bytes=40 * 1024 * 1024),
    )(expert_u, expert_v)


def _peer_kernel(idx_ref, idxn_ref, g_ref, h_ref, x1_ref, mod_ref, gfin_ref, tab_ref, tabw_ref, o_ref,
                 buf_ref, sem_ref, *, tb, nsel, d):
    i = pl.program_id(0)
    last = pl.num_programs(0) - 1
    slot = i % 2
    nslot = 1 - slot
    nct = d // LANES
    tok_tiles = (nsel // SUBLANES) * nct
    slot_tiles = tb * tok_tiles

    def row_copy(row, sl, t, e):
        first = sl * slot_tiles + t * tok_tiles + (e // SUBLANES) * nct
        return pltpu.make_async_copy(
            tab_ref.at[row], buf_ref.at[pl.ds(first, nct), e % SUBLANES, :], sem_ref.at[sl])

    def slot_wait(sl):
        pltpu.make_async_copy(
            tabw_ref.at[pl.ds(0, slot_tiles)], buf_ref.at[pl.ds(sl * slot_tiles, slot_tiles)], sem_ref.at[sl]).wait()

    def issue(idx_r, t, sl):
        for e in range(nsel):
            row_copy(idx_r[t, e], sl, t, e).start(priority=e % 2)

    @pl.when(i == 0)
    def _():
        def body(t, c):
            issue(idx_ref, t, 0)
            return c
        lax.fori_loop(0, tb, body, 0)

    slot_wait(slot)

    nt = (((1,), (1,)), ((), ()))
    rows8 = lax.broadcasted_iota(_I32, (SUBLANES, 2 * nsel), 0)

    def token_matrix(t):
        first = pl.multiple_of(slot * slot_tiles + t * tok_tiles, tok_tiles)
        w = buf_ref[pl.ds(first, tok_tiles)]
        rows = [jnp.concatenate([w[g * nct + c] for c in range(nct)], axis=1) for g in range(nsel // SUBLANES)]
        return pltpu.bitcast(jnp.concatenate(rows, axis=0), _BF16)

    def group(grp, carry):
        r0 = pl.multiple_of(grp * SUBLANES, SUBLANES)
        hb = h_ref[pl.ds(r0, SUBLANES), :].astype(_BF16)
        zsel = jnp.zeros((SUBLANES, 2 * nsel), _F32)
        for j in range(SUBLANES):
            issue(idxn_ref, r0 + j, nslot)
            zt = lax.dot_general(hb, token_matrix(r0 + j), nt, preferred_element_type=_F32)
            zsel = jnp.where(rows8 == j, zt, zsel)
        act = pltpu.roll(_gelu(zsel) * g_ref[pl.ds(r0, SUBLANES), :], 1, 1)
        out = jnp.zeros((SUBLANES, d), _F32)
        for j in range(SUBLANES):
            aj = jnp.where(rows8 == j, act, 0.0).astype(_BF16)
            out = out + _bdot(aj, token_matrix(r0 + j))
        x2 = x1_ref[pl.ds(r0, SUBLANES), :] + mod_ref[5:6, :] * out
        o_ref[pl.ds(r0, SUBLANES), :] = _rms(x2, gfin_ref[...])
        return carry

    lax.fori_loop(0, tb // SUBLANES, group, 0)

    @pl.when(i == last)
    def _():
        slot_wait(nslot)


def _peer_call(idx, g2, h2, x1, mod3, g_final, tab, tb, seq):
    t_total, nsel = idx.shape
    d = h2.shape[-1]
    nblk = t_total // tb
    nct = d // LANES
    n_exp = tab.shape[0]
    tab3 = tab.reshape(n_exp, nct, LANES)
    tabw = tab.reshape(n_exp * nct // SUBLANES, SUBLANES, LANES)
    slot_tiles = tb * (nsel // SUBLANES) * nct
    kern = functools.partial(_peer_kernel, tb=tb, nsel=nsel, d=d)
    return pl.pallas_call(
        kern,
        grid=(nblk,),
        in_specs=[
            pl.BlockSpec((tb, nsel), lambda i: (i, 0), memory_space=pltpu.SMEM),
            pl.BlockSpec((tb, nsel), lambda i: (jnp.minimum(i + 1, nblk - 1), 0), memory_space=pltpu.SMEM),
            pl.BlockSpec((tb, 2 * nsel), lambda i: (i, 0)),
            pl.BlockSpec((tb, d), lambda i: (i, 0)),
            pl.BlockSpec((tb, d), lambda i: (i, 0)),
            pl.BlockSpec((None, 6, d), lambda i: ((i * tb) // seq, 0, 0)),
            pl.BlockSpec((1, d), lambda i: (0, 0)),
            pl.BlockSpec(memory_space=pl.ANY),
            pl.BlockSpec(memory_space=pl.ANY),
        ],
        out_specs=pl.BlockSpec((tb, d), lambda i: (i, 0)),
        out_shape=jax.ShapeDtypeStruct((t_total, d), _F32),
        scratch_shapes=[
            pltpu.VMEM((2 * slot_tiles, SUBLANES, LANES), jnp.uint32),
            pltpu.SemaphoreType.DMA((2,)),
        ],
        compiler_params=pltpu.CompilerParams(
            dimension_semantics=("arbitrary",), vmem_limit_bytes=56 * 1024 * 1024),
    )(idx, idx, g2, h2, x1, mod3, g_final, tab3, tabw)


def _layer(x, mod3, g_norm_mix, w_in, conv_w, conv_b, w_gate_a, b_gate_a, w_gate_i, b_gate_i, lru_lambda, g_v,
           w_spatial, b_spatial, w_out, g_norm_ffn, w_query, sub_keys, expert_u, expert_v, g_out, ts, tq, tb):
    bsz, seq, d = x.shape
    d_lru = conv_w.shape[-1]
    d_sgu = g_v.shape[-1]
    y = _mixer_call(
        x, mod3, g_norm_mix.reshape(1, d), w_in.astype(_BF16), conv_w, conv_b.reshape(1, d_lru),
        w_gate_a.astype(_BF16), b_gate_a.reshape(1, d_lru), w_gate_i.astype(_BF16), b_gate_i.reshape(1, d_lru),
        lru_lambda.reshape(1, d_lru), g_v.reshape(1, d_sgu), w_spatial, b_spatial.T, ts)
    x1, h2, idx_t, g_t = _route_call(
        y, x, mod3, w_out.astype(_BF16), g_norm_ffn.reshape(1, d), w_query.astype(_BF16), sub_keys.astype(_BF16), tq)
    t_total = bsz * seq
    nsel = idx_t.shape[0]
    idx = idx_t.T
    g_tok = g_t.T
    g2 = jnp.stack([g_tok, jnp.zeros_like(g_tok)], axis=-1).reshape(t_total, 2 * nsel)
    tab = _pack_call(expert_u, expert_v, min(256, expert_u.shape[0]))
    out = _peer_call(idx, g2, h2.reshape(t_total, d), x1.reshape(t_total, d), mod3, g_out.reshape(1, d), tab, tb, seq)
    return out.reshape(bsz, seq, d)


def kernel(x, c, w_ada, b_ada, g_norm_mix, w_in, conv_w, conv_b, w_gate_a, b_gate_a, w_gate_i, b_gate_i,
           lru_lambda, g_v, w_spatial, b_spatial, w_out, g_norm_ffn, w_query, sub_keys, expert_u, expert_v, g_final):
    depth = w_ada.shape[0]
    assert depth == 1, "the final RMSNorm is fused into the layer's last call"
    bsz, seq, d = x.shape
    ts = min(256, seq)
    tq = min(256, seq)
    tb = 16
    n_mod = w_ada.shape[-1]
    mod = _ada_call(c, w_ada[0], b_ada[0], 1024 if n_mod % 1024 == 0 else n_mod)
    mod3 = mod.reshape(bsz, 6, d)
    return _layer(x, mod3, g_norm_mix[0], w_in[0], conv_w[0], conv_b[0], w_gate_a[0], b_gate_a[0], w_gate_i[0],
                  b_gate_i[0], lru_lambda[0], g_v[0], w_spatial[0], b_spatial[0], w_out[0], g_norm_ffn[0],
                  w_query[0], sub_keys[0], expert_u[0], expert_v[0], g_final, ts, tq, tb)
```

```python
import functools

import jax
import jax.numpy as jnp
from jax import lax
from jax.experimental import pallas as pl
from jax.experimental.pallas import tpu as pltpu

_F32 = jnp.float32
_BF16 = jnp.bfloat16
_I32 = jnp.int32

EPS = 1e-6
LRU_C = 8.0
CONV_WIDTH = 4
PEER_TOPK = 16
LANES = 128
SUBLANES = 8
_NEG_INF = float("-inf")


def _gelu(x):
    return 0.5 * x * (1.0 + jnp.tanh(0.7978845608028654 * (x + 0.044715 * (x * x * x))))


def _softplus(y):
    return jnp.maximum(y, 0.0) + jnp.log1p(jnp.exp(-jnp.abs(y)))


def _rms(x, g):
    return x * lax.rsqrt(jnp.mean(x * x, axis=-1, keepdims=True) + EPS) * g


def _bdot(a, b):
    return jnp.dot(a, b, preferred_element_type=_F32)


def _ada_kernel(c_ref, w_ref, b_ref, o_ref):
    c = c_ref[...]
    a = c * jax.nn.sigmoid(c)
    o_ref[...] = _bdot(a.astype(_BF16), w_ref[...].astype(_BF16)) + b_ref[...]


def _ada_call(c, w, b, tn):
    bsz, d = c.shape
    n = w.shape[1]
    return pl.pallas_call(
        _ada_kernel,
        grid=(n // tn,),
        in_specs=[
            pl.BlockSpec((bsz, d), lambda j: (0, 0)),
            pl.BlockSpec((d, tn), lambda j: (0, j)),
            pl.BlockSpec((1, tn), lambda j: (0, j)),
        ],
        out_specs=pl.BlockSpec((bsz, tn), lambda j: (0, j)),
        out_shape=jax.ShapeDtypeStruct((bsz, n), _F32),
        compiler_params=pltpu.CompilerParams(
            dimension_semantics=("arbitrary",), vmem_limit_bytes=40 * 1024 * 1024),
    )(c, w, b.reshape(1, n))


def _mixer_kernel(x_ref, mod_ref, gmix_ref, win_ref, cw_ref, cb_ref, wa_ref, ba_ref, wi_ref, bi_ref,
                  lam_ref, gv_ref, wsp_ref, bspt_ref, o_ref, ext_ref, hprev_ref, *, ts, d_lru, heads, groups, chunk):
    s = pl.program_id(1)

    @pl.when(s == 0)
    def _():
        ext_ref[0:SUBLANES, :] = jnp.zeros((SUBLANES, d_lru), _F32)
        hprev_ref[...] = jnp.zeros_like(hprev_ref)

    x = x_ref[...]
    shift = mod_ref[0:1, :]
    scale = mod_ref[1:2, :]
    h = _rms(x, gmix_ref[...]) * (1.0 + scale) + shift
    hb = h.astype(_BF16)

    x_lru = _bdot(hb, win_ref[:, 0:d_lru])
    ext_ref[SUBLANES:SUBLANES + ts, :] = x_lru
    xc = cb_ref[...] + cw_ref[3:4, :] * x_lru
    for k in range(CONV_WIDTH - 1):
        off = SUBLANES - (CONV_WIDTH - 1) + k
        xc = xc + cw_ref[k:k + 1, :] * ext_ref[off:off + ts, :]
    ext_ref[0:SUBLANES, :] = x_lru[ts - SUBLANES:ts, :]

    xcb = xc.astype(_BF16)
    blk = d_lru // heads
    ra, ia = [], []
    for hh in range(heads):
        xh = xcb[:, hh * blk:(hh + 1) * blk]
        ra.append(_bdot(xh, wa_ref[hh]))
        ia.append(_bdot(xh, wi_ref[hh]))
    r = jax.nn.sigmoid(jnp.concatenate(ra, axis=1) + ba_ref[...])
    ig = jax.nn.sigmoid(jnp.concatenate(ia, axis=1) + bi_ref[...])
    log_a = (-LRU_C) * r * _softplus(-lam_ref[...])
    a = jnp.exp(log_a)
    bv = jnp.sqrt(-jnp.tanh(log_a) * (a * a + 1.0)) * (ig * xc)

    row = lax.broadcasted_iota(_I32, (ts, d_lru), 0)
    dist = 1
    while dist < ts:
        a_s = pltpu.roll(a, dist, 0)
        b_s = pltpu.roll(bv, dist, 0)
        m = row >= dist
        bv = jnp.where(m, a * b_s + bv, bv)
        a = jnp.where(m, a * a_s, a)
        dist *= 2
    hs = bv + a * hprev_ref[0:1, :]
    hprev_ref[0:1, :] = hs[ts - 1:ts, :]

    y_gate = _bdot(hb, win_ref[:, d_lru:2 * d_lru])
    o_ref[:, 0:d_lru] = (hs * _gelu(y_gate)).astype(_BF16)

    d_sgu = gv_ref.shape[1]
    gd = d_sgu // groups
    u = _gelu(_bdot(hb, win_ref[:, 2 * d_lru:2 * d_lru + d_sgu]))
    v = _gelu(_bdot(hb, win_ref[:, 2 * d_lru + d_sgu:2 * d_lru + 2 * d_sgu]))
    vnb = _rms(v, gv_ref[...]).astype(_BF16)
    tri = lax.broadcasted_iota(_I32, (chunk, chunk), 0) >= lax.broadcasted_iota(_I32, (chunk, chunk), 1)
    for g in range(groups):
        wm = jnp.where(tri, wsp_ref[g], 0.0).astype(_BF16)
        bcol = bspt_ref[:, g:g + 1]
        for n in range(ts // chunk):
            sg = _bdot(wm, vnb[n * chunk:(n + 1) * chunk, g * gd:(g + 1) * gd]) + bcol
            ug = u[n * chunk:(n + 1) * chunk, g * gd:(g + 1) * gd]
            o_ref[n * chunk:(n + 1) * chunk, d_lru + g * gd:d_lru + (g + 1) * gd] = (ug * sg).astype(_BF16)


def _mixer_call(x, mod3, g_mix, w_in_b, conv_w, conv_b, w_a_b, b_a, w_i_b, b_i, lam, g_v, w_sp, b_sp_t, ts):
    bsz, seq, d = x.shape
    d_lru = conv_w.shape[1]
    d_sgu = g_v.shape[1]
    heads = w_a_b.shape[0]
    groups, chunk, _ = w_sp.shape
    d_mix = d_lru + d_sgu
    const = lambda shape: pl.BlockSpec(shape, lambda b, s: (0,) * len(shape), pipeline_mode=pl.Buffered(1))
    kern = functools.partial(_mixer_kernel, ts=ts, d_lru=d_lru, heads=heads, groups=groups, chunk=chunk)
    return pl.pallas_call(
        kern,
        grid=(bsz, seq // ts),
        in_specs=[
            pl.BlockSpec((None, ts, d), lambda b, s: (b, s, 0)),
            pl.BlockSpec((None, 6, d), lambda b, s: (b, 0, 0)),
            const((1, d)),
            const(w_in_b.shape),
            const(conv_w.shape),
            const((1, d_lru)),
            const(w_a_b.shape),
            const((1, d_lru)),
            const(w_i_b.shape),
            const((1, d_lru)),
            const((1, d_lru)),
            const((1, d_sgu)),
            const(w_sp.shape),
            const(b_sp_t.shape),
        ],
        out_specs=pl.BlockSpec((None, ts, d_mix), lambda b, s: (b, s, 0)),
        out_shape=jax.ShapeDtypeStruct((bsz, seq, d_mix), _BF16),
        scratch_shapes=[pltpu.VMEM((ts + SUBLANES, d_lru), _F32), pltpu.VMEM((SUBLANES, d_lru), _F32)],
        compiler_params=pltpu.CompilerParams(
            dimension_semantics=("arbitrary", "arbitrary"), vmem_limit_bytes=56 * 1024 * 1024),
    )(x, mod3, g_mix, w_in_b, conv_w, conv_b, w_a_b, b_a, w_i_b, b_i, lam, g_v, w_sp, b_sp_t)


def _topk_cols(vals, ids, k):
    n, t = vals.shape
    pos = lax.broadcasted_iota(_I32, (n, t), 0)
    kio = lax.broadcasted_iota(_I32, (k, t), 0)
    out_v = jnp.zeros((k, t), _F32)
    out_i = jnp.zeros((k, t), _I32)
    for j in range(k):
        m = jnp.max(vals, axis=0, keepdims=True)
        p = jnp.min(jnp.where(vals == m, pos, n), axis=0, keepdims=True)
        hit = pos == p
        if ids is None:
            e = p
        else:
            e = jnp.max(jnp.where(hit, ids, -1), axis=0, keepdims=True)
        out_v = jnp.where(kio == j, m, out_v)
        out_i = jnp.where(kio == j, e, out_i)
        vals = jnp.where(hit, _NEG_INF, vals)
    return out_v, out_i


def _route_kernel(y_ref, x_ref, mod_ref, wout_ref, gffn_ref, wq_ref, keys_ref,
                  x1_ref, h2_ref, idx_ref, g_ref, *, heads, n_keys):
    x1 = x_ref[...] + mod_ref[2:3, :] * _bdot(y_ref[...], wout_ref[...])
    x1_ref[...] = x1
    h2 = _rms(x1, gffn_ref[...]) * (1.0 + mod_ref[4:5, :]) + mod_ref[3:4, :]
    h2_ref[...] = h2
    q = _bdot(h2.astype(_BF16), wq_ref[...]).astype(_BF16)
    dk = keys_ref.shape[-1]
    nt = (((1,), (1,)), ((), ()))
    for hh in range(heads):
        tops = []
        for p in range(2):
            col = (hh * 2 + p) * dk
            st = lax.dot_general(keys_ref[hh, p], q[:, col:col + dk], nt, preferred_element_type=_F32)
            tops.append(_topk_cols(st, None, PEER_TOPK))
        (v1, i1), (v2, i2) = tops
        cand = jnp.concatenate([v1[i:i + 1, :] + v2 for i in range(PEER_TOPK)], axis=0)
        cidx = jnp.concatenate([i1[i:i + 1, :] * n_keys + i2 for i in range(PEER_TOPK)], axis=0)
        tv, ti = _topk_cols(cand, cidx, PEER_TOPK)
        ex = jnp.exp(tv - tv[0:1, :])
        gsm = ex / jnp.sum(ex, axis=0, keepdims=True)
        idx_ref[hh * PEER_TOPK:(hh + 1) * PEER_TOPK, :] = ti
        g_ref[hh * PEER_TOPK:(hh + 1) * PEER_TOPK, :] = gsm


def _route_call(y, x, mod3, w_out_b, g_ffn, w_q_b, keys_b, tq):
    bsz, seq, d = x.shape
    d_mix = y.shape[-1]
    heads, _, n_keys, dk = keys_b.shape
    nsel = heads * PEER_TOPK
    nq = seq // tq
    const = lambda shape: pl.BlockSpec(shape, lambda b, s: (0,) * len(shape), pipeline_mode=pl.Buffered(1))
    kern = functools.partial(_route_kernel, heads=heads, n_keys=n_keys)
    return pl.pallas_call(
        kern,
        grid=(bsz, nq),
        in_specs=[
            pl.BlockSpec((None, tq, d_mix), lambda b, s: (b, s, 0)),
            pl.BlockSpec((None, tq, d), lambda b, s: (b, s, 0)),
            pl.BlockSpec((None, 6, d), lambda b, s: (b, 0, 0)),
            const(w_out_b.shape),
            const((1, d)),
            const(w_q_b.shape),
            const(keys_b.shape),
        ],
        out_specs=[
            pl.BlockSpec((None, tq, d), lambda b, s: (b, s, 0)),
            pl.BlockSpec((None, tq, d), lambda b, s: (b, s, 0)),
            pl.BlockSpec((nsel, tq), lambda b, s: (0, b * nq + s)),
            pl.BlockSpec((nsel, tq), lambda b, s: (0, b * nq + s)),
        ],
        out_shape=[
            jax.ShapeDtypeStruct((bsz, seq, d), _F32),
            jax.ShapeDtypeStruct((bsz, seq, d), _F32),
            jax.ShapeDtypeStruct((nsel, bsz * seq), _I32),
            jax.ShapeDtypeStruct((nsel, bsz * seq), _F32),
        ],
        compiler_params=pltpu.CompilerParams(
            dimension_semantics=("arbitrary", "arbitrary"), vmem_limit_bytes=56 * 1024 * 1024),
    )(y, x, mod3, w_out_b, g_ffn, w_q_b, keys_b)


def _pack_kernel(u_ref, v_ref, o_ref):
    o_ref[...] = pltpu.pack_elementwise([u_ref[...], v_ref[...]], packed_dtype=_BF16)


def _pack_call(expert_u, expert_v, rows):
    n, d = expert_u.shape
    spec = pl.BlockSpec((rows, d), lambda i: (i, 0))
    return pl.pallas_call(
        _pack_kernel,
        grid=(n // rows,),
        in_specs=[spec, spec],
        out_specs=spec,
        out_shape=jax.ShapeDtypeStruct((n, d), jnp.uint32),
        compiler_params=pltpu.CompilerParams(
            dimension_semantics=("arbitrary",), vmem_limit_bytes=40 * 1024 * 1024),
    )(expert_u, expert_v)


def _peer_kernel(idx_ref, idxn_ref, g_ref, h_ref, x1_ref, mod_ref, gfin_ref, tab_ref, tabw_ref, o_ref,
                 buf0_ref, buf1_ref, sem_ref, *, tb, nsel, d, nblk):
    i = pl.program_id(0)
    nct = d // LANES
    tok_tiles = (nsel // SUBLANES) * nct
    bufs = (buf0_ref, buf1_ref)

    def row_copy(row, sl, t, e):
        first = t * tok_tiles + (e // SUBLANES) * nct
        return pltpu.make_async_copy(
            tab_ref.at[row], bufs[sl].at[pl.ds(first, nct), e % SUBLANES, :], sem_ref.at[sl, t])

    def token_wait(sl, t):
        pltpu.make_async_copy(
            tabw_ref.at[pl.ds(0, tok_tiles)], bufs[sl].at[pl.ds(t * tok_tiles, tok_tiles)], sem_ref.at[sl, t]).wait()

    def issue(idx_r, sl, t, e_lo, e_hi):
        for e in range(e_lo, e_hi):
            row_copy(idx_r[t, e], sl, t, e).start(priority=e % 2)

    @pl.when(i == 0)
    def _():
        def body(t, c):
            issue(idx_ref, 0, t, 0, nsel)
            return c
        lax.fori_loop(0, tb, body, 0)

    ngrp = nsel // SUBLANES
    rows8 = lax.broadcasted_iota(_I32, (SUBLANES, 2 * nsel), 0)
    lane = lax.broadcasted_iota(_I32, (nsel, LANES), 1)
    spread = (lax.broadcasted_iota(_I32, (nsel, 2 * nsel), 1)
              == 2 * lax.broadcasted_iota(_I32, (nsel, 2 * nsel), 0) + 1).astype(_BF16)

    def token_matrix(sl, t):
        w = bufs[sl][t * tok_tiles:(t + 1) * tok_tiles]
        rows = [jnp.concatenate([w[g * nct + c] for c in range(nct)], axis=1) for g in range(ngrp)]
        return pltpu.bitcast(jnp.concatenate(rows, axis=0), _BF16)

    def u_dots(sl, t):
        acc = [jnp.zeros((SUBLANES, LANES), _F32) for _ in range(ngrp)]
        for c in range(nct):
            xc = jnp.broadcast_to(h_ref[t:t + 1, c * LANES:(c + 1) * LANES], (SUBLANES, LANES))
            for g in range(ngrp):
                w = bufs[sl][t * tok_tiles + g * nct + c]
                u = pltpu.unpack_elementwise(w, index=0, packed_dtype=_BF16, unpacked_dtype=_F32)
                acc[g] = acc[g] + u * xc
        return jnp.sum(jnp.concatenate(acc, axis=0), axis=1, keepdims=True)

    def evaluate(sl):
        nsl = 1 - sl
        for r0 in range(0, tb, SUBLANES):
            zcols = jnp.zeros((nsel, LANES), _F32)
            for j in range(SUBLANES):
                token_wait(sl, r0 + j)
                issue(idxn_ref, nsl, r0 + j, 0, nsel // 2)
                zcols = jnp.where(lane == j, u_dots(sl, r0 + j), zcols)
            z = zcols.T[0:SUBLANES, :]
            act = (_gelu(z) * g_ref[r0:r0 + SUBLANES, :]).astype(_BF16)
            act2 = _bdot(act, spread)
            out = jnp.zeros((SUBLANES, d), _F32)
            for j in range(SUBLANES):
                issue(idxn_ref, nsl, r0 + j, nsel // 2, nsel)
                aj = jnp.where(rows8 == j, act2, 0.0).astype(_BF16)
                out = out + _bdot(aj, token_matrix(sl, r0 + j))
            x2 = x1_ref[r0:r0 + SUBLANES, :] + mod_ref[5:6, :] * out
            o_ref[r0:r0 + SUBLANES, :] = _rms(x2, gfin_ref[...])

    for sl in range(2):
        @pl.when(i % 2 == sl)
        def _(sl=sl):
            evaluate(sl)

    @pl.when(i == nblk - 1)
    def _():
        for t in range(tb):
            token_wait(nblk % 2, t)


def _peer_call(idx, g2, h2, x1, mod3, g_final, tab, tb, seq):
    t_total, nsel = idx.shape
    d = h2.shape[-1]
    nblk = t_total // tb
    nct = d // LANES
    n_exp = tab.shape[0]
    tab3 = tab.reshape(n_exp, nct, LANES)
    tabw = tab.reshape(n_exp * nct // SUBLANES, SUBLANES, LANES)
    slot_tiles = tb * (nsel // SUBLANES) * nct
    kern = functools.partial(_peer_kernel, tb=tb, nsel=nsel, d=d, nblk=nblk)
    return pl.pallas_call(
        kern,
        grid=(nblk,),
        in_specs=[
            pl.BlockSpec((tb, nsel), lambda i: (i, 0), memory_space=pltpu.SMEM),
            pl.BlockSpec((tb, nsel), lambda i: (jnp.minimum(i + 1, nblk - 1), 0), memory_space=pltpu.SMEM),
            pl.BlockSpec((tb, nsel), lambda i: (i, 0)),
            pl.BlockSpec((tb, d), lambda i: (i, 0)),
            pl.BlockSpec((tb, d), lambda i: (i, 0)),
            pl.BlockSpec((None, 6, d), lambda i: ((i * tb) // seq, 0, 0)),
            pl.BlockSpec((1, d), lambda i: (0, 0)),
            pl.BlockSpec(memory_space=pl.ANY),
            pl.BlockSpec(memory_space=pl.ANY),
        ],
        out_specs=pl.BlockSpec((tb, d), lambda i: (i, 0)),
        out_shape=jax.ShapeDtypeStruct((t_total, d), _F32),
        scratch_shapes=[
            pltpu.VMEM((slot_tiles, SUBLANES, LANES), jnp.uint32),
            pltpu.VMEM((slot_tiles, SUBLANES, LANES), jnp.uint32),
            pltpu.SemaphoreType.DMA((2, tb)),
        ],
        compiler_params=pltpu.CompilerParams(
            dimension_semantics=("arbitrary",), vmem_limit_bytes=56 * 1024 * 1024),
    )(idx, idx, g2, h2, x1, mod3, g_final, tab3, tabw)


def _layer(x, mod3, g_norm_mix, w_in, conv_w, conv_b, w_gate_a, b_gate_a, w_gate_i, b_gate_i, lru_lambda, g_v,
           w_spatial, b_spatial, w_out, g_norm_ffn, w_query, sub_keys, expert_u, expert_v, g_out, ts, tq, tb):
    bsz, seq, d = x.shape
    d_lru = conv_w.shape[-1]
    d_sgu = g_v.shape[-1]
    y = _mixer_call(
        x, mod3, g_norm_mix.reshape(1, d), w_in.astype(_BF16), conv_w, conv_b.reshape(1, d_lru),
        w_gate_a.astype(_BF16), b_gate_a.reshape(1, d_lru), w_gate_i.astype(_BF16), b_gate_i.reshape(1, d_lru),
        lru_lambda.reshape(1, d_lru), g_v.reshape(1, d_sgu), w_spatial, b_spatial.T, ts)
    x1, h2, idx_t, g_t = _route_call(
        y, x, mod3, w_out.astype(_BF16), g_norm_ffn.reshape(1, d), w_query.astype(_BF16), sub_keys.astype(_BF16), tq)
    t_total = bsz * seq
    nsel = idx_t.shape[0]
    idx = idx_t.T
    g2 = g_t.T
    tab = _pack_call(expert_u, expert_v, min(256, expert_u.shape[0]))
    out = _peer_call(idx, g2, h2.reshape(t_total, d), x1.reshape(t_total, d), mod3, g_out.reshape(1, d), tab, tb, seq)
    return out.reshape(bsz, seq, d)


def kernel(x, c, w_ada, b_ada, g_norm_mix, w_in, conv_w, conv_b, w_gate_a, b_gate_a, w_gate_i, b_gate_i,
           lru_lambda, g_v, w_spatial, b_spatial, w_out, g_norm_ffn, w_query, sub_keys, expert_u, expert_v, g_final):
    depth = w_ada.shape[0]
    assert depth == 1, "the final RMSNorm is fused into the layer's last call"
    bsz, seq, d = x.shape
    ts = min(256, seq)
    tq = min(256, seq)
    tb = 16
    n_mod = w_ada.shape[-1]
    mod = _ada_call(c, w_ada[0], b_ada[0], 1024 if n_mod % 1024 == 0 else n_mod)
    mod3 = mod.reshape(bsz, 6, d)
    return _layer(x, mod3, g_norm_mix[0], w_in[0], conv_w[0], conv_b[0], w_gate_a[0], b_gate_a[0], w_gate_i[0],
                  b_gate_i[0], lru_lambda[0], g_v[0], w_spatial[0], b_spatial[0], w_out[0], g_norm_ffn[0],
                  w_query[0], sub_keys[0], expert_u[0], expert_v[0], g_final, ts, tq, tb)
```

```python
import functools

import jax
import jax.numpy as jnp
from jax import lax
from jax.experimental import pallas as pl
from jax.experimental.pallas import tpu as pltpu

_F32 = jnp.float32
_BF16 = jnp.bfloat16
_I32 = jnp.int32

EPS = 1e-6
LRU_C = 8.0
CONV_WIDTH = 4
PEER_TOPK = 16
LANES = 128
SUBLANES = 8
_NEG_INF = float("-inf")
_POS_INF = float("inf")


def _gelu(x):
    return 0.5 * x * (1.0 + jnp.tanh(0.7978845608028654 * (x + 0.044715 * (x * x * x))))


def _softplus(y):
    return jnp.maximum(y, 0.0) + jnp.log1p(jnp.exp(-jnp.abs(y)))


def _rms(x, g):
    return x * lax.rsqrt(jnp.mean(x * x, axis=-1, keepdims=True) + EPS) * g


def _bdot(a, b):
    return jnp.dot(a, b, preferred_element_type=_F32)


def _ada_kernel(c_ref, w_ref, b_ref, o_ref):
    c = c_ref[...]
    a = c * jax.nn.sigmoid(c)
    o_ref[...] = _bdot(a.astype(_BF16), w_ref[...].astype(_BF16)) + b_ref[...]


def _ada_call(c, w, b, tn):
    bsz, d = c.shape
    n = w.shape[1]
    return pl.pallas_call(
        _ada_kernel,
        grid=(n // tn,),
        in_specs=[
            pl.BlockSpec((bsz, d), lambda j: (0, 0)),
            pl.BlockSpec((d, tn), lambda j: (0, j)),
            pl.BlockSpec((1, tn), lambda j: (0, j)),
        ],
        out_specs=pl.BlockSpec((bsz, tn), lambda j: (0, j)),
        out_shape=jax.ShapeDtypeStruct((bsz, n), _F32),
        compiler_params=pltpu.CompilerParams(
            dimension_semantics=("arbitrary",), vmem_limit_bytes=40 * 1024 * 1024),
    )(c, w, b.reshape(1, n))


def _mixer_kernel(x_ref, mod_ref, gmix_ref, win_ref, cw_ref, cb_ref, wa_ref, ba_ref, wi_ref, bi_ref,
                  lam_ref, gv_ref, wsp_ref, bspt_ref, o_ref, ext_ref, hprev_ref, *, ts, d_lru, heads, groups, chunk):
    s = pl.program_id(1)

    @pl.when(s == 0)
    def _():
        ext_ref[0:SUBLANES, :] = jnp.zeros((SUBLANES, d_lru), _F32)
        hprev_ref[...] = jnp.zeros_like(hprev_ref)

    x = x_ref[...]
    shift = mod_ref[0:1, :]
    scale = mod_ref[1:2, :]
    h = _rms(x, gmix_ref[...]) * (1.0 + scale) + shift
    hb = h.astype(_BF16)

    x_lru = _bdot(hb, win_ref[:, 0:d_lru])
    ext_ref[SUBLANES:SUBLANES + ts, :] = x_lru
    xc = cb_ref[...] + cw_ref[3:4, :] * x_lru
    for k in range(CONV_WIDTH - 1):
        off = SUBLANES - (CONV_WIDTH - 1) + k
        xc = xc + cw_ref[k:k + 1, :] * ext_ref[off:off + ts, :]
    ext_ref[0:SUBLANES, :] = x_lru[ts - SUBLANES:ts, :]

    xcb = xc.astype(_BF16)
    blk = d_lru // heads
    ra, ia = [], []
    for hh in range(heads):
        xh = xcb[:, hh * blk:(hh + 1) * blk]
        ra.append(_bdot(xh, wa_ref[hh]))
        ia.append(_bdot(xh, wi_ref[hh]))
    r = jax.nn.sigmoid(jnp.concatenate(ra, axis=1) + ba_ref[...])
    ig = jax.nn.sigmoid(jnp.concatenate(ia, axis=1) + bi_ref[...])
    log_a = (-LRU_C) * r * _softplus(-lam_ref[...])
    a = jnp.exp(log_a)
    bv = jnp.sqrt(-jnp.tanh(log_a) * (a * a + 1.0)) * (ig * xc)

    row = lax.broadcasted_iota(_I32, (ts, d_lru), 0)
    dist = 1
    while dist < ts:
        a_s = pltpu.roll(a, dist, 0)
        b_s = pltpu.roll(bv, dist, 0)
        m = row >= dist
        bv = jnp.where(m, a * b_s + bv, bv)
        a = jnp.where(m, a * a_s, a)
        dist *= 2
    hs = bv + a * hprev_ref[0:1, :]
    hprev_ref[0:1, :] = hs[ts - 1:ts, :]

    y_gate = _bdot(hb, win_ref[:, d_lru:2 * d_lru])
    o_ref[:, 0:d_lru] = (hs * _gelu(y_gate)).astype(_BF16)

    d_sgu = gv_ref.shape[1]
    gd = d_sgu // groups
    u = _gelu(_bdot(hb, win_ref[:, 2 * d_lru:2 * d_lru + d_sgu]))
    v = _gelu(_bdot(hb, win_ref[:, 2 * d_lru + d_sgu:2 * d_lru + 2 * d_sgu]))
    vnb = _rms(v, gv_ref[...]).astype(_BF16)
    tri = lax.broadcasted_iota(_I32, (chunk, chunk), 0) >= lax.broadcasted_iota(_I32, (chunk, chunk), 1)
    for g in range(groups):
        wm = jnp.where(tri, wsp_ref[g], 0.0).astype(_BF16)
        bcol = bspt_ref[:, g:g + 1]
        for n in range(ts // chunk):
            sg = _bdot(wm, vnb[n * chunk:(n + 1) * chunk, g * gd:(g + 1) * gd]) + bcol
            ug = u[n * chunk:(n + 1) * chunk, g * gd:(g + 1) * gd]
            o_ref[n * chunk:(n + 1) * chunk, d_lru + g * gd:d_lru + (g + 1) * gd] = (ug * sg).astype(_BF16)


def _mixer_call(x, mod3, g_mix, w_in_b, conv_w, conv_b, w_a_b, b_a, w_i_b, b_i, lam, g_v, w_sp, b_sp_t, ts):
    bsz, seq, d = x.shape
    d_lru = conv_w.shape[1]
    d_sgu = g_v.shape[1]
    heads = w_a_b.shape[0]
    groups, chunk, _ = w_sp.shape
    d_mix = d_lru + d_sgu
    const = lambda shape: pl.BlockSpec(shape, lambda b, s: (0,) * len(shape), pipeline_mode=pl.Buffered(1))
    kern = functools.partial(_mixer_kernel, ts=ts, d_lru=d_lru, heads=heads, groups=groups, chunk=chunk)
    return pl.pallas_call(
        kern,
        grid=(bsz, seq // ts),
        in_specs=[
            pl.BlockSpec((None, ts, d), lambda b, s: (b, s, 0)),
            pl.BlockSpec((None, 6, d), lambda b, s: (b, 0, 0)),
            const((1, d)),
            const(w_in_b.shape),
            const(conv_w.shape),
            const((1, d_lru)),
            const(w_a_b.shape),
            const((1, d_lru)),
            const(w_i_b.shape),
            const((1, d_lru)),
            const((1, d_lru)),
            const((1, d_sgu)),
            const(w_sp.shape),
            const(b_sp_t.shape),
        ],
        out_specs=pl.BlockSpec((None, ts, d_mix), lambda b, s: (b, s, 0)),
        out_shape=jax.ShapeDtypeStruct((bsz, seq, d_mix), _BF16),
        scratch_shapes=[pltpu.VMEM((ts + SUBLANES, d_lru), _F32), pltpu.VMEM((SUBLANES, d_lru), _F32)],
        compiler_params=pltpu.CompilerParams(
            dimension_semantics=("arbitrary", "arbitrary"), vmem_limit_bytes=56 * 1024 * 1024),
    )(x, mod3, g_mix, w_in_b, conv_w, conv_b, w_a_b, b_a, w_i_b, b_i, lam, g_v, w_sp, b_sp_t)


def _topk_cols(vals, pos, ids, k):
    n, t = vals.shape
    kio = lax.broadcasted_iota(_I32, (k, t), 0)
    out_v = jnp.zeros((k, t), _F32)
    out_i = jnp.zeros((k, t), _F32)
    for j in range(k):
        m = jnp.max(vals, axis=0, keepdims=True)
        p = jnp.min(jnp.where(vals == m, pos, _POS_INF), axis=0, keepdims=True)
        hit = pos == p
        e = p if ids is None else jnp.max(jnp.where(hit, ids, -1.0), axis=0, keepdims=True)
        out_v = jnp.where(kio == j, m, out_v)
        out_i = jnp.where(kio == j, e, out_i)
        vals = jnp.where(hit, _NEG_INF, vals)
    return out_v, out_i


def _pair_candidates(v1, i1, v2, i2, n_keys):
    k, t = v1.shape
    half = k // 2
    jrow = lax.broadcasted_iota(_I32, (half, t), 0)
    vals = [v1[0:1, :] + v2]
    pos = [lax.broadcasted_iota(_I32, (k, t), 0).astype(_F32)]
    ids = [i1[0:1, :] * n_keys + i2]
    for i in range(1, half):
        keep = jrow < k // (i + 1)
        vals.append(jnp.where(keep, v1[i:i + 1, :] + v2[0:half, :], _NEG_INF))
        pos.append((jrow + i * k).astype(_F32))
        ids.append(i1[i:i + 1, :] * n_keys + i2[0:half, :])
    vals.append(v1[half:k, :] + v2[0:1, :])
    pos.append(((jrow + half) * k).astype(_F32))
    ids.append(i1[half:k, :] * n_keys + i2[0:1, :])
    return jnp.concatenate(vals, axis=0), jnp.concatenate(pos, axis=0), jnp.concatenate(ids, axis=0)


def _route_kernel(y_ref, x_ref, mod_ref, wout_ref, gffn_ref, wq_ref, keys_ref,
                  x1_ref, h2_ref, idx_ref, g_ref, *, heads, n_keys):
    x1 = x_ref[...] + mod_ref[2:3, :] * _bdot(y_ref[...], wout_ref[...])
    x1_ref[...] = x1
    h2 = _rms(x1, gffn_ref[...]) * (1.0 + mod_ref[4:5, :]) + mod_ref[3:4, :]
    h2_ref[...] = h2
    q = _bdot(h2.astype(_BF16), wq_ref[...]).astype(_BF16)
    dk = keys_ref.shape[-1]
    nt = (((1,), (1,)), ((), ()))
    key_pos = lax.broadcasted_iota(_I32, (n_keys, q.shape[0]), 0).astype(_F32)
    for hh in range(heads):
        tops = []
        for p in range(2):
            col = (hh * 2 + p) * dk
            st = lax.dot_general(keys_ref[hh, p], q[:, col:col + dk], nt, preferred_element_type=_F32)
            tops.append(_topk_cols(st, key_pos, None, PEER_TOPK))
        (v1, i1), (v2, i2) = tops
        cand, cpos, cids = _pair_candidates(v1, i1, v2, i2, float(n_keys))
        tv, ti = _topk_cols(cand, cpos, cids, PEER_TOPK)
        ex = jnp.exp(tv - tv[0:1, :])
        gsm = ex / jnp.sum(ex, axis=0, keepdims=True)
        idx_ref[hh * PEER_TOPK:(hh + 1) * PEER_TOPK, :] = ti.astype(_I32)
        g_ref[hh * PEER_TOPK:(hh + 1) * PEER_TOPK, :] = gsm


def _route_call(y, x, mod3, w_out_b, g_ffn, w_q_b, keys_b, tq):
    bsz, seq, d = x.shape
    d_mix = y.shape[-1]
    heads, _, n_keys, dk = keys_b.shape
    nsel = heads * PEER_TOPK
    nq = seq // tq
    const = lambda shape: pl.BlockSpec(shape, lambda b, s: (0,) * len(shape), pipeline_mode=pl.Buffered(1))
    kern = functools.partial(_route_kernel, heads=heads, n_keys=n_keys)
    return pl.pallas_call(
        kern,
        grid=(bsz, nq),
        in_specs=[
            pl.BlockSpec((None, tq, d_mix), lambda b, s: (b, s, 0)),
            pl.BlockSpec((None, tq, d), lambda b, s: (b, s, 0)),
            pl.BlockSpec((None, 6, d), lambda b, s: (b, 0, 0)),
            const(w_out_b.shape),
            const((1, d)),
            const(w_q_b.shape),
            const(keys_b.shape),
        ],
        out_specs=[
            pl.BlockSpec((None, tq, d), lambda b, s: (b, s, 0)),
            pl.BlockSpec((None, tq, d), lambda b, s: (b, s, 0)),
            pl.BlockSpec((nsel, tq), lambda b, s: (0, b * nq + s)),
            pl.BlockSpec((nsel, tq), lambda b, s: (0, b * nq + s)),
        ],
        out_shape=[
            jax.ShapeDtypeStruct((bsz, seq, d), _F32),
            jax.ShapeDtypeStruct((bsz, seq, d), _F32),
            jax.ShapeDtypeStruct((nsel, bsz * seq), _I32),
            jax.ShapeDtypeStruct((nsel, bsz * seq), _F32),
        ],
        compiler_params=pltpu.CompilerParams(
            dimension_semantics=("arbitrary", "arbitrary"), vmem_limit_bytes=56 * 1024 * 1024),
    )(y, x, mod3, w_out_b, g_ffn, w_q_b, keys_b)


def _pack_kernel(u_ref, v_ref, o_ref):
    o_ref[...] = pltpu.pack_elementwise([u_ref[...], v_ref[...]], packed_dtype=_BF16)


def _pack_call(expert_u, expert_v, rows):
    n, d = expert_u.shape
    spec = pl.BlockSpec((rows, d), lambda i: (i, 0))
    return pl.pallas_call(
        _pack_kernel,
        grid=(n // rows,),
        in_specs=[spec, spec],
        out_specs=spec,
        out_shape=jax.ShapeDtypeStruct((n, d), jnp.uint32),
        compiler_params=pltpu.CompilerParams(
            dimension_semantics=("arbitrary",), vmem_limit_bytes=40 * 1024 * 1024),
    )(expert_u, expert_v)


def _peer_kernel(idx_ref, idxn_ref, g_ref, h_ref, x1_ref, mod_ref, gfin_ref, tab_ref, tabw_ref, o_ref,
                 buf0_ref, buf1_ref, sem_ref, *, tb, nsel, d, nblk):
    i = pl.program_id(0)
    nct = d // LANES
    tok_tiles = (nsel // SUBLANES) * nct
    bufs = (buf0_ref, buf1_ref)

    def row_copy(row, sl, t, e):
        first = t * tok_tiles + (e // SUBLANES) * nct
        return pltpu.make_async_copy(
            tab_ref.at[row], bufs[sl].at[pl.ds(first, nct), e % SUBLANES, :], sem_ref.at[sl, t])

    def token_wait(sl, t):
        pltpu.make_async_copy(
            tabw_ref.at[pl.ds(0, tok_tiles)], bufs[sl].at[pl.ds(t * tok_tiles, tok_tiles)], sem_ref.at[sl, t]).wait()

    def issue(idx_r, sl, t, e_lo, e_hi):
        for e in range(e_lo, e_hi):
            row_copy(idx_r[t, e], sl, t, e).start(priority=e % 2)

    @pl.when(i == 0)
    def _():
        def body(t, c):
            issue(idx_ref, 0, t, 0, nsel)
            return c
        lax.fori_loop(0, tb, body, 0)

    ngrp = nsel // SUBLANES
    rows8 = lax.broadcasted_iota(_I32, (SUBLANES, 2 * nsel), 0)
    lane = lax.broadcasted_iota(_I32, (nsel, LANES), 1)
    spread = (lax.broadcasted_iota(_I32, (nsel, 2 * nsel), 1)
              == 2 * lax.broadcasted_iota(_I32, (nsel, 2 * nsel), 0) + 1).astype(_BF16)

    def token_matrix(sl, t):
        w = bufs[sl][t * tok_tiles:(t + 1) * tok_tiles]
        rows = [jnp.concatenate([w[g * nct + c] for c in range(nct)], axis=1) for g in range(ngrp)]
        return pltpu.bitcast(jnp.concatenate(rows, axis=0), _BF16)

    def u_dots(sl, t):
        acc = [jnp.zeros((SUBLANES, LANES), _F32) for _ in range(ngrp)]
        for c in range(nct):
            xc = jnp.broadcast_to(h_ref[t:t + 1, c * LANES:(c + 1) * LANES], (SUBLANES, LANES))
            for g in range(ngrp):
                w = bufs[sl][t * tok_tiles + g * nct + c]
                u = pltpu.unpack_elementwise(w, index=0, packed_dtype=_BF16, unpacked_dtype=_F32)
                acc[g] = acc[g] + u * xc
        return jnp.sum(jnp.concatenate(acc, axis=0), axis=1, keepdims=True)

    def evaluate(sl):
        nsl = 1 - sl
        ngroups = tb // SUBLANES
        copies = [(t, e) for t in range(tb) for e in range(nsel)]
        nregions = (ngroups + 1) * SUBLANES
        share = -(-len(copies) // nregions)
        region = 0
        act2 = None
        for p in range(ngroups + 1):
            r0, v0 = p * SUBLANES, (p - 1) * SUBLANES
            zcols = jnp.zeros((nsel, LANES), _F32)
            out = jnp.zeros((SUBLANES, d), _F32)
            for j in range(SUBLANES):
                if p < ngroups:
                    token_wait(sl, r0 + j)
                for t, e in copies[region * share:(region + 1) * share]:
                    row_copy(idxn_ref[t, e], nsl, t, e).start(priority=e % 2)
                region += 1
                if p > 0:
                    aj = jnp.where(rows8 == j, act2, 0.0).astype(_BF16)
                    out = out + _bdot(aj, token_matrix(sl, v0 + j))
                if p < ngroups:
                    zcols = jnp.where(lane == j, u_dots(sl, r0 + j), zcols)
            if p > 0:
                x2 = x1_ref[v0:v0 + SUBLANES, :] + mod_ref[5:6, :] * out
                o_ref[v0:v0 + SUBLANES, :] = _rms(x2, gfin_ref[...])
            if p < ngroups:
                z = zcols.T[0:SUBLANES, :]
                act = (_gelu(z) * g_ref[r0:r0 + SUBLANES, :]).astype(_BF16)
                act2 = _bdot(act, spread)

    for sl in range(2):
        @pl.when(i % 2 == sl)
        def _(sl=sl):
            evaluate(sl)

    @pl.when(i == nblk - 1)
    def _():
        for t in range(tb):
            token_wait(nblk % 2, t)


def _peer_call(idx, g2, h2, x1, mod3, g_final, tab, tb, seq):
    t_total, nsel = idx.shape
    d = h2.shape[-1]
    nblk = t_total // tb
    nct = d // LANES
    n_exp = tab.shape[0]
    tab3 = tab.reshape(n_exp, nct, LANES)
    tabw = tab.reshape(n_exp * nct // SUBLANES, SUBLANES, LANES)
    slot_tiles = tb * (nsel // SUBLANES) * nct
    kern = functools.partial(_peer_kernel, tb=tb, nsel=nsel, d=d, nblk=nblk)
    return pl.pallas_call(
        kern,
        grid=(nblk,),
        in_specs=[
            pl.BlockSpec((tb, nsel), lambda i: (i, 0), memory_space=pltpu.SMEM),
            pl.BlockSpec((tb, nsel), lambda i: (jnp.minimum(i + 1, nblk - 1), 0), memory_space=pltpu.SMEM),
            pl.BlockSpec((tb, nsel), lambda i: (i, 0)),
            pl.BlockSpec((tb, d), lambda i: (i, 0)),
            pl.BlockSpec((tb, d), lambda i: (i, 0)),
            pl.BlockSpec((None, 6, d), lambda i: ((i * tb) // seq, 0, 0)),
            pl.BlockSpec((1, d), lambda i: (0, 0)),
            pl.BlockSpec(memory_space=pl.ANY),
            pl.BlockSpec(memory_space=pl.ANY),
        ],
        out_specs=pl.BlockSpec((tb, d), lambda i: (i, 0)),
        out_shape=jax.ShapeDtypeStruct((t_total, d), _F32),
        scratch_shapes=[
            pltpu.VMEM((slot_tiles, SUBLANES, LANES), jnp.uint32),
            pltpu.VMEM((slot_tiles, SUBLANES, LANES), jnp.uint32),
            pltpu.SemaphoreType.DMA((2, tb)),
        ],
        compiler_params=pltpu.CompilerParams(
            dimension_semantics=("arbitrary",), vmem_limit_bytes=56 * 1024 * 1024),
    )(idx, idx, g2, h2, x1, mod3, g_final, tab3, tabw)


def _layer(x, mod3, g_norm_mix, w_in, conv_w, conv_b, w_gate_a, b_gate_a, w_gate_i, b_gate_i, lru_lambda, g_v,
           w_spatial, b_spatial, w_out, g_norm_ffn, w_query, sub_keys, expert_u, expert_v, g_out, ts, tq, tb):
    bsz, seq, d = x.shape
    d_lru = conv_w.shape[-1]
    d_sgu = g_v.shape[-1]
    y = _mixer_call(
        x, mod3, g_norm_mix.reshape(1, d), w_in.astype(_BF16), conv_w, conv_b.reshape(1, d_lru),
        w_gate_a.astype(_BF16), b_gate_a.reshape(1, d_lru), w_gate_i.astype(_BF16), b_gate_i.reshape(1, d_lru),
        lru_lambda.reshape(1, d_lru), g_v.reshape(1, d_sgu), w_spatial, b_spatial.T, ts)
    x1, h2, idx_t, g_t = _route_call(
        y, x, mod3, w_out.astype(_BF16), g_norm_ffn.reshape(1, d), w_query.astype(_BF16), sub_keys.astype(_BF16), tq)
    t_total = bsz * seq
    nsel = idx_t.shape[0]
    idx = idx_t.T
    g2 = g_t.T
    tab = _pack_call(expert_u, expert_v, min(256, expert_u.shape[0]))
    out = _peer_call(idx, g2, h2.reshape(t_total, d), x1.reshape(t_total, d), mod3, g_out.reshape(1, d), tab, tb, seq)
    return out.reshape(bsz, seq, d)


def kernel(x, c, w_ada, b_ada, g_norm_mix, w_in, conv_w, conv_b, w_gate_a, b_gate_a, w_gate_i, b_gate_i,
           lru_lambda, g_v, w_spatial, b_spatial, w_out, g_norm_ffn, w_query, sub_keys, expert_u, expert_v, g_final):
    depth = w_ada.shape[0]
    assert depth == 1, "the final RMSNorm is fused into the layer's last call"
    bsz, seq, d = x.shape
    ts = min(256, seq)
    tq = min(256, seq)
    tb = 16
    n_mod = w_ada.shape[-1]
    mod = _ada_call(c, w_ada[0], b_ada[0], 1024 if n_mod % 1024 == 0 else n_mod)
    mod3 = mod.reshape(bsz, 6, d)
    return _layer(x, mod3, g_norm_mix[0], w_in[0], conv_w[0], conv_b[0], w_gate_a[0], b_gate_a[0], w_gate_i[0],
                  b_gate_i[0], lru_lambda[0], g_v[0], w_spatial[0], b_spatial[0], w_out[0], g_norm_ffn[0],
                  w_query[0], sub_keys[0], expert_u[0], expert_v[0], g_final, ts, tq, tb)
```

```python
import functools

import jax
import jax.numpy as jnp
from jax import lax
from jax.experimental import pallas as pl
from jax.experimental.pallas import tpu as pltpu

_F32 = jnp.float32
_BF16 = jnp.bfloat16
_I32 = jnp.int32

EPS = 1e-6
LRU_C = 8.0
CONV_WIDTH = 4
PEER_TOPK = 16
LANES = 128
SUBLANES = 8
_NEG_INF = float("-inf")
_POS_INF = float("inf")


def _gelu(x):
    return 0.5 * x * (1.0 + jnp.tanh(0.7978845608028654 * (x + 0.044715 * (x * x * x))))


def _softplus(y):
    return jnp.maximum(y, 0.0) + jnp.log1p(jnp.exp(-jnp.abs(y)))


def _rms(x, g):
    return x * lax.rsqrt(jnp.mean(x * x, axis=-1, keepdims=True) + EPS) * g


def _bdot(a, b):
    return jnp.dot(a, b, preferred_element_type=_F32)


def _ada_kernel(c_ref, w_ref, b_ref, o_ref):
    c = c_ref[...]
    a = c * jax.nn.sigmoid(c)
    o_ref[...] = _bdot(a.astype(_BF16), w_ref[...].astype(_BF16)) + b_ref[...]


def _ada_call(c, w, b, tn):
    bsz, d = c.shape
    n = w.shape[1]
    return pl.pallas_call(
        _ada_kernel,
        grid=(n // tn,),
        in_specs=[
            pl.BlockSpec((bsz, d), lambda j: (0, 0)),
            pl.BlockSpec((d, tn), lambda j: (0, j)),
            pl.BlockSpec((1, tn), lambda j: (0, j)),
        ],
        out_specs=pl.BlockSpec((bsz, tn), lambda j: (0, j)),
        out_shape=jax.ShapeDtypeStruct((bsz, n), _F32),
        compiler_params=pltpu.CompilerParams(
            dimension_semantics=("arbitrary",), vmem_limit_bytes=40 * 1024 * 1024),
    )(c, w, b.reshape(1, n))


def _mixer_kernel(x_ref, mod_ref, gmix_ref, win_ref, cw_ref, cb_ref, wa_ref, ba_ref, wi_ref, bi_ref,
                  lam_ref, gv_ref, wsp_ref, bspt_ref, o_ref, ext_ref, hprev_ref, *, ts, d_lru, heads, groups, chunk):
    s = pl.program_id(1)

    @pl.when(s == 0)
    def _():
        ext_ref[0:SUBLANES, :] = jnp.zeros((SUBLANES, d_lru), _F32)
        hprev_ref[...] = jnp.zeros_like(hprev_ref)

    x = x_ref[...]
    shift = mod_ref[0:1, :]
    scale = mod_ref[1:2, :]
    h = _rms(x, gmix_ref[...]) * (1.0 + scale) + shift
    hb = h.astype(_BF16)

    x_lru = _bdot(hb, win_ref[:, 0:d_lru])
    ext_ref[SUBLANES:SUBLANES + ts, :] = x_lru
    xc = cb_ref[...] + cw_ref[3:4, :] * x_lru
    for k in range(CONV_WIDTH - 1):
        off = SUBLANES - (CONV_WIDTH - 1) + k
        xc = xc + cw_ref[k:k + 1, :] * ext_ref[off:off + ts, :]
    ext_ref[0:SUBLANES, :] = x_lru[ts - SUBLANES:ts, :]

    xcb = xc.astype(_BF16)
    blk = d_lru // heads
    ra, ia = [], []
    for hh in range(heads):
        xh = xcb[:, hh * blk:(hh + 1) * blk]
        ra.append(_bdot(xh, wa_ref[hh]))
        ia.append(_bdot(xh, wi_ref[hh]))
    r = jax.nn.sigmoid(jnp.concatenate(ra, axis=1) + ba_ref[...])
    ig = jax.nn.sigmoid(jnp.concatenate(ia, axis=1) + bi_ref[...])
    log_a = (-LRU_C) * r * _softplus(-lam_ref[...])
    a = jnp.exp(log_a)
    bv = jnp.sqrt(-jnp.tanh(log_a) * (a * a + 1.0)) * (ig * xc)

    row = lax.broadcasted_iota(_I32, (ts, d_lru), 0)
    dist = 1
    while dist < ts:
        a_s = pltpu.roll(a, dist, 0)
        b_s = pltpu.roll(bv, dist, 0)
        m = row >= dist
        bv = jnp.where(m, a * b_s + bv, bv)
        a = jnp.where(m, a * a_s, a)
        dist *= 2
    hs = bv + a * hprev_ref[0:1, :]
    hprev_ref[0:1, :] = hs[ts - 1:ts, :]

    y_gate = _bdot(hb, win_ref[:, d_lru:2 * d_lru])
    o_ref[:, 0:d_lru] = (hs * _gelu(y_gate)).astype(_BF16)

    d_sgu = gv_ref.shape[1]
    gd = d_sgu // groups
    u = _gelu(_bdot(hb, win_ref[:, 2 * d_lru:2 * d_lru + d_sgu]))
    v = _gelu(_bdot(hb, win_ref[:, 2 * d_lru + d_sgu:2 * d_lru + 2 * d_sgu]))
    vnb = _rms(v, gv_ref[...]).astype(_BF16)
    tri = lax.broadcasted_iota(_I32, (chunk, chunk), 0) >= lax.broadcasted_iota(_I32, (chunk, chunk), 1)
    for g in range(groups):
        wm = jnp.where(tri, wsp_ref[g], 0.0).astype(_BF16)
        bcol = bspt_ref[:, g:g + 1]
        for n in range(ts // chunk):
            sg = _bdot(wm, vnb[n * chunk:(n + 1) * chunk, g * gd:(g + 1) * gd]) + bcol
            ug = u[n * chunk:(n + 1) * chunk, g * gd:(g + 1) * gd]
            o_ref[n * chunk:(n + 1) * chunk, d_lru + g * gd:d_lru + (g + 1) * gd] = (ug * sg).astype(_BF16)


def _mixer_call(x, mod3, g_mix, w_in_b, conv_w, conv_b, w_a_b, b_a, w_i_b, b_i, lam, g_v, w_sp, b_sp_t, ts):
    bsz, seq, d = x.shape
    d_lru = conv_w.shape[1]
    d_sgu = g_v.shape[1]
    heads = w_a_b.shape[0]
    groups, chunk, _ = w_sp.shape
    d_mix = d_lru + d_sgu
    const = lambda shape: pl.BlockSpec(shape, lambda b, s: (0,) * len(shape), pipeline_mode=pl.Buffered(1))
    kern = functools.partial(_mixer_kernel, ts=ts, d_lru=d_lru, heads=heads, groups=groups, chunk=chunk)
    return pl.pallas_call(
        kern,
        grid=(bsz, seq // ts),
        in_specs=[
            pl.BlockSpec((None, ts, d), lambda b, s: (b, s, 0)),
            pl.BlockSpec((None, 6, d), lambda b, s: (b, 0, 0)),
            const((1, d)),
            const(w_in_b.shape),
            const(conv_w.shape),
            const((1, d_lru)),
            const(w_a_b.shape),
            const((1, d_lru)),
            const(w_i_b.shape),
            const((1, d_lru)),
            const((1, d_lru)),
            const((1, d_sgu)),
            const(w_sp.shape),
            const(b_sp_t.shape),
        ],
        out_specs=pl.BlockSpec((None, ts, d_mix), lambda b, s: (b, s, 0)),
        out_shape=jax.ShapeDtypeStruct((bsz, seq, d_mix), _BF16),
        scratch_shapes=[pltpu.VMEM((ts + SUBLANES, d_lru), _F32), pltpu.VMEM((SUBLANES, d_lru), _F32)],
        compiler_params=pltpu.CompilerParams(
            dimension_semantics=("arbitrary", "arbitrary"), vmem_limit_bytes=56 * 1024 * 1024),
    )(x, mod3, g_mix, w_in_b, conv_w, conv_b, w_a_b, b_a, w_i_b, b_i, lam, g_v, w_sp, b_sp_t)


def _topk_cols(vals, pos, ids, k):
    n, t = vals.shape
    kio = lax.broadcasted_iota(_I32, (k, t), 0)
    out_v = jnp.zeros((k, t), _F32)
    out_i = jnp.zeros((k, t), _F32)
    for j in range(k):
        m = jnp.max(vals, axis=0, keepdims=True)
        p = jnp.min(jnp.where(vals == m, pos, _POS_INF), axis=0, keepdims=True)
        hit = pos == p
        e = p if ids is None else jnp.max(jnp.where(hit, ids, -1.0), axis=0, keepdims=True)
        out_v = jnp.where(kio == j, m, out_v)
        out_i = jnp.where(kio == j, e, out_i)
        vals = jnp.where(hit, _NEG_INF, vals)
    return out_v, out_i


def _pair_candidates(v1, i1, v2, i2, n_keys):
    k, t = v1.shape
    half = k // 2
    jrow = lax.broadcasted_iota(_I32, (half, t), 0)
    vals = [v1[0:1, :] + v2]
    pos = [lax.broadcasted_iota(_I32, (k, t), 0).astype(_F32)]
    ids = [i1[0:1, :] * n_keys + i2]
    for i in range(1, half):
        keep = jrow < k // (i + 1)
        vals.append(jnp.where(keep, v1[i:i + 1, :] + v2[0:half, :], _NEG_INF))
        pos.append((jrow + i * k).astype(_F32))
        ids.append(i1[i:i + 1, :] * n_keys + i2[0:half, :])
    vals.append(v1[half:k, :] + v2[0:1, :])
    pos.append(((jrow + half) * k).astype(_F32))
    ids.append(i1[half:k, :] * n_keys + i2[0:1, :])
    return jnp.concatenate(vals, axis=0), jnp.concatenate(pos, axis=0), jnp.concatenate(ids, axis=0)


def _route_kernel(y_ref, x_ref, mod_ref, wout_ref, gffn_ref, wq_ref, keys_ref,
                  x1_ref, h2_ref, idx_ref, g_ref, *, heads, n_keys):
    x1 = x_ref[...] + mod_ref[2:3, :] * _bdot(y_ref[...], wout_ref[...])
    x1_ref[...] = x1
    h2 = _rms(x1, gffn_ref[...]) * (1.0 + mod_ref[4:5, :]) + mod_ref[3:4, :]
    h2_ref[...] = h2
    q = _bdot(h2.astype(_BF16), wq_ref[...]).astype(_BF16)
    dk = keys_ref.shape[-1]
    nt = (((1,), (1,)), ((), ()))
    key_pos = lax.broadcasted_iota(_I32, (n_keys, q.shape[0]), 0).astype(_F32)
    for hh in range(heads):
        tops = []
        for p in range(2):
            col = (hh * 2 + p) * dk
            st = lax.dot_general(keys_ref[hh, p], q[:, col:col + dk], nt, preferred_element_type=_F32)
            tops.append(_topk_cols(st, key_pos, None, PEER_TOPK))
        (v1, i1), (v2, i2) = tops
        cand, cpos, cids = _pair_candidates(v1, i1, v2, i2, float(n_keys))
        tv, ti = _topk_cols(cand, cpos, cids, PEER_TOPK)
        ex = jnp.exp(tv - tv[0:1, :])
        gsm = ex / jnp.sum(ex, axis=0, keepdims=True)
        idx_ref[hh * PEER_TOPK:(hh + 1) * PEER_TOPK, :] = ti.astype(_I32)
        g_ref[hh * PEER_TOPK:(hh + 1) * PEER_TOPK, :] = gsm


def _route_call(y, x, mod3, w_out_b, g_ffn, w_q_b, keys_b, tq):
    bsz, seq, d = x.shape
    d_mix = y.shape[-1]
    heads, _, n_keys, dk = keys_b.shape
    nsel = heads * PEER_TOPK
    nq = seq // tq
    const = lambda shape: pl.BlockSpec(shape, lambda b, s: (0,) * len(shape), pipeline_mode=pl.Buffered(1))
    kern = functools.partial(_route_kernel, heads=heads, n_keys=n_keys)
    return pl.pallas_call(
        kern,
        grid=(bsz, nq),
        in_specs=[
            pl.BlockSpec((None, tq, d_mix), lambda b, s: (b, s, 0)),
            pl.BlockSpec((None, tq, d), lambda b, s: (b, s, 0)),
            pl.BlockSpec((None, 6, d), lambda b, s: (b, 0, 0)),
            const(w_out_b.shape),
            const((1, d)),
            const(w_q_b.shape),
            const(keys_b.shape),
        ],
        out_specs=[
            pl.BlockSpec((None, tq, d), lambda b, s: (b, s, 0)),
            pl.BlockSpec((None, tq, d), lambda b, s: (b, s, 0)),
            pl.BlockSpec((nsel, tq), lambda b, s: (0, b * nq + s)),
            pl.BlockSpec((nsel, tq), lambda b, s: (0, b * nq + s)),
        ],
        out_shape=[
            jax.ShapeDtypeStruct((bsz, seq, d), _F32),
            jax.ShapeDtypeStruct((bsz, seq, d), _F32),
            jax.ShapeDtypeStruct((nsel, bsz * seq), _I32),
            jax.ShapeDtypeStruct((nsel, bsz * seq), _F32),
        ],
        compiler_params=pltpu.CompilerParams(
            dimension_semantics=("arbitrary", "arbitrary"), vmem_limit_bytes=56 * 1024 * 1024),
    )(y, x, mod3, w_out_b, g_ffn, w_q_b, keys_b)


def _pack_kernel(u_ref, v_ref, o_ref):
    o_ref[...] = pltpu.pack_elementwise([u_ref[...], v_ref[...]], packed_dtype=_BF16)


def _pack_call(expert_u, expert_v, rows):
    n, d = expert_u.shape
    spec = pl.BlockSpec((rows, d), lambda i: (i, 0))
    return pl.pallas_call(
        _pack_kernel,
        grid=(n // rows,),
        in_specs=[spec, spec],
        out_specs=spec,
        out_shape=jax.ShapeDtypeStruct((n, d), jnp.uint32),
        compiler_params=pltpu.CompilerParams(
            dimension_semantics=("arbitrary",), vmem_limit_bytes=40 * 1024 * 1024),
    )(expert_u, expert_v)


def _peer_kernel(idx_ref, idxn_ref, g_ref, h_ref, x1_ref, mod_ref, gfin_ref, tab_ref, tabw_ref, o_ref,
                 buf0_ref, buf1_ref, sem_ref, zero_ref, *, tb, nsel, d, nblk):
    i = pl.program_id(0)
    nct = d // LANES
    tok_tiles = (nsel // SUBLANES) * nct
    bufs = (buf0_ref, buf1_ref)

    def row_copy(row, sl, t, e):
        first = t * tok_tiles + (e // SUBLANES) * nct
        return pltpu.make_async_copy(
            tab_ref.at[row], bufs[sl].at[pl.ds(first, nct), e % SUBLANES, :], sem_ref.at[sl, t])

    def token_wait(sl, t):
        pltpu.make_async_copy(
            tabw_ref.at[pl.ds(0, tok_tiles)], bufs[sl].at[pl.ds(t * tok_tiles, tok_tiles)], sem_ref.at[sl, t]).wait()

    def issue(idx_r, sl, t, e_lo, e_hi):
        for e in range(e_lo, e_hi):
            row_copy(idx_r[t, e], sl, t, e).start(priority=e % 2)

    @pl.when(i == 0)
    def _():
        zero_ref[...] = jnp.zeros_like(zero_ref)

        def body(t, c):
            issue(idx_ref, 0, t, 0, nsel)
            return c
        lax.fori_loop(0, tb, body, 0)

    ngrp = nsel // SUBLANES
    rows8 = lax.broadcasted_iota(_I32, (SUBLANES, 2 * nsel), 0)
    lane = lax.broadcasted_iota(_I32, (nsel, LANES), 1)
    spread = (lax.broadcasted_iota(_I32, (nsel, 2 * nsel), 1)
              == 2 * lax.broadcasted_iota(_I32, (nsel, 2 * nsel), 0) + 1).astype(_BF16)

    def token_matrix(sl, t):
        w = bufs[sl][t * tok_tiles:(t + 1) * tok_tiles]
        rows = [jnp.concatenate([w[g * nct + c] for c in range(nct)], axis=1) for g in range(ngrp)]
        return pltpu.bitcast(jnp.concatenate(rows, axis=0), _BF16)

    def u_dots(sl, t):
        acc = [jnp.zeros((SUBLANES, LANES), _F32) for _ in range(ngrp)]
        for c in range(nct):
            xc = h_ref[t:t + 1, c * LANES:(c + 1) * LANES] + zero_ref[...]
            for g in range(ngrp):
                w = bufs[sl][t * tok_tiles + g * nct + c]
                u = pltpu.unpack_elementwise(w, index=0, packed_dtype=_BF16, unpacked_dtype=_F32)
                acc[g] = acc[g] + u * xc
        return jnp.sum(jnp.concatenate(acc, axis=0), axis=1, keepdims=True)

    def evaluate(sl):
        nsl = 1 - sl
        ngroups = tb // SUBLANES
        copies = [(t, e) for t in range(tb) for e in range(nsel)]
        nregions = (ngroups + 1) * SUBLANES
        share = -(-len(copies) // nregions)
        region = 0
        act2 = None
        for p in range(ngroups + 1):
            r0, v0 = p * SUBLANES, (p - 1) * SUBLANES
            zcols = jnp.zeros((nsel, LANES), _F32)
            out = jnp.zeros((SUBLANES, d), _F32)
            for j in range(SUBLANES):
                if p < ngroups:
                    token_wait(sl, r0 + j)
                for t, e in copies[region * share:(region + 1) * share]:
                    row_copy(idxn_ref[t, e], nsl, t, e).start(priority=e % 2)
                region += 1
                if p > 0:
                    aj = jnp.where(rows8 == j, act2, 0.0).astype(_BF16)
                    out = out + _bdot(aj, token_matrix(sl, v0 + j))
                if p < ngroups:
                    zcols = jnp.where(lane == j, u_dots(sl, r0 + j), zcols)
            if p > 0:
                x2 = x1_ref[v0:v0 + SUBLANES, :] + mod_ref[5:6, :] * out
                o_ref[v0:v0 + SUBLANES, :] = _rms(x2, gfin_ref[...])
            if p < ngroups:
                z = zcols.T[0:SUBLANES, :]
                act = (_gelu(z) * g_ref[r0:r0 + SUBLANES, :]).astype(_BF16)
                act2 = _bdot(act, spread)

    for sl in range(2):
        @pl.when(i % 2 == sl)
        def _(sl=sl):
            evaluate(sl)

    @pl.when(i == nblk - 1)
    def _():
        for t in range(tb):
            token_wait(nblk % 2, t)


def _peer_call(idx, g2, h2, x1, mod3, g_final, tab, tb, seq):
    t_total, nsel = idx.shape
    d = h2.shape[-1]
    nblk = t_total // tb
    nct = d // LANES
    n_exp = tab.shape[0]
    tab3 = tab.reshape(n_exp, nct, LANES)
    tabw = tab.reshape(n_exp * nct // SUBLANES, SUBLANES, LANES)
    slot_tiles = tb * (nsel // SUBLANES) * nct
    kern = functools.partial(_peer_kernel, tb=tb, nsel=nsel, d=d, nblk=nblk)
    return pl.pallas_call(
        kern,
        grid=(nblk,),
        in_specs=[
            pl.BlockSpec((tb, nsel), lambda i: (i, 0), memory_space=pltpu.SMEM),
            pl.BlockSpec((tb, nsel), lambda i: (jnp.minimum(i + 1, nblk - 1), 0), memory_space=pltpu.SMEM),
            pl.BlockSpec((tb, nsel), lambda i: (i, 0)),
            pl.BlockSpec((tb, d), lambda i: (i, 0)),
            pl.BlockSpec((tb, d), lambda i: (i, 0)),
            pl.BlockSpec((None, 6, d), lambda i: ((i * tb) // seq, 0, 0)),
            pl.BlockSpec((1, d), lambda i: (0, 0)),
            pl.BlockSpec(memory_space=pl.ANY),
            pl.BlockSpec(memory_space=pl.ANY),
        ],
        out_specs=pl.BlockSpec((tb, d), lambda i: (i, 0)),
        out_shape=jax.ShapeDtypeStruct((t_total, d), _F32),
        scratch_shapes=[
            pltpu.VMEM((slot_tiles, SUBLANES, LANES), jnp.uint32),
            pltpu.VMEM((slot_tiles, SUBLANES, LANES), jnp.uint32),
            pltpu.SemaphoreType.DMA((2, tb)),
            pltpu.VMEM((SUBLANES, LANES), _F32),
        ],
        compiler_params=pltpu.CompilerParams(
            dimension_semantics=("arbitrary",), vmem_limit_bytes=56 * 1024 * 1024),
    )(idx, idx, g2, h2, x1, mod3, g_final, tab3, tabw)


def _layer(x, mod3, g_norm_mix, w_in, conv_w, conv_b, w_gate_a, b_gate_a, w_gate_i, b_gate_i, lru_lambda, g_v,
           w_spatial, b_spatial, w_out, g_norm_ffn, w_query, sub_keys, expert_u, expert_v, g_out, ts, tq, tb):
    bsz, seq, d = x.shape
    d_lru = conv_w.shape[-1]
    d_sgu = g_v.shape[-1]
    y = _mixer_call(
        x, mod3, g_norm_mix.reshape(1, d), w_in.astype(_BF16), conv_w, conv_b.reshape(1, d_lru),
        w_gate_a.astype(_BF16), b_gate_a.reshape(1, d_lru), w_gate_i.astype(_BF16), b_gate_i.reshape(1, d_lru),
        lru_lambda.reshape(1, d_lru), g_v.reshape(1, d_sgu), w_spatial, b_spatial.T, ts)
    x1, h2, idx_t, g_t = _route_call(
        y, x, mod3, w_out.astype(_BF16), g_norm_ffn.reshape(1, d), w_query.astype(_BF16), sub_keys.astype(_BF16), tq)
    t_total = bsz * seq
    nsel = idx_t.shape[0]
    idx = idx_t.T
    g2 = g_t.T
    tab = _pack_call(expert_u, expert_v, min(256, expert_u.shape[0]))
    out = _peer_call(idx, g2, h2.reshape(t_total, d), x1.reshape(t_total, d), mod3, g_out.reshape(1, d), tab, tb, seq)
    return out.reshape(bsz, seq, d)


def kernel(x, c, w_ada, b_ada, g_norm_mix, w_in, conv_w, conv_b, w_gate_a, b_gate_a, w_gate_i, b_gate_i,
           lru_lambda, g_v, w_spatial, b_spatial, w_out, g_norm_ffn, w_query, sub_keys, expert_u, expert_v, g_final):
    depth = w_ada.shape[0]
    assert depth == 1, "the final RMSNorm is fused into the layer's last call"
    bsz, seq, d = x.shape
    ts = min(256, seq)
    tq = min(256, seq)
    tb = 16
    n_mod = w_ada.shape[-1]
    mod = _ada_call(c, w_ada[0], b_ada[0], 1024 if n_mod % 1024 == 0 else n_mod)
    mod3 = mod.reshape(bsz, 6, d)
    return _layer(x, mod3, g_norm_mix[0], w_in[0], conv_w[0], conv_b[0], w_gate_a[0], b_gate_a[0], w_gate_i[0],
                  b_gate_i[0], lru_lambda[0], g_v[0], w_spatial[0], b_spatial[0], w_out[0], g_norm_ffn[0],
                  w_query[0], sub_keys[0], expert_u[0], expert_v[0], g_final, ts, tq, tb)
```

```python
import functools

import jax
import jax.numpy as jnp
from jax import lax
from jax.experimental import pallas as pl
from jax.experimental.pallas import tpu as pltpu

_F32 = jnp.float32
_BF16 = jnp.bfloat16
_I32 = jnp.int32

EPS = 1e-6
LRU_C = 8.0
CONV_WIDTH = 4
PEER_TOPK = 16
LANES = 128
SUBLANES = 8
_NEG_INF = float("-inf")
_POS_INF = float("inf")


def _gelu(x):
    return 0.5 * x * (1.0 + jnp.tanh(0.7978845608028654 * (x + 0.044715 * (x * x * x))))


def _softplus(y):
    return jnp.maximum(y, 0.0) + jnp.log1p(jnp.exp(-jnp.abs(y)))


def _rms(x, g):
    return x * lax.rsqrt(jnp.mean(x * x, axis=-1, keepdims=True) + EPS) * g


def _bdot(a, b):
    return jnp.dot(a, b, preferred_element_type=_F32)


def _ada_kernel(c_ref, w_ref, b_ref, o_ref):
    c = c_ref[...]
    a = c * jax.nn.sigmoid(c)
    o_ref[...] = _bdot(a.astype(_BF16), w_ref[...].astype(_BF16)) + b_ref[...]


def _ada_call(c, w, b, tn):
    bsz, d = c.shape
    n = w.shape[1]
    return pl.pallas_call(
        _ada_kernel,
        grid=(n // tn,),
        in_specs=[
            pl.BlockSpec((bsz, d), lambda j: (0, 0)),
            pl.BlockSpec((d, tn), lambda j: (0, j)),
            pl.BlockSpec((1, tn), lambda j: (0, j)),
        ],
        out_specs=pl.BlockSpec((bsz, tn), lambda j: (0, j)),
        out_shape=jax.ShapeDtypeStruct((bsz, n), _F32),
        compiler_params=pltpu.CompilerParams(
            dimension_semantics=("arbitrary",), vmem_limit_bytes=40 * 1024 * 1024),
    )(c, w, b.reshape(1, n))


def _mixer_kernel(x_ref, mod_ref, gmix_ref, win_ref, cw_ref, cb_ref, wa_ref, ba_ref, wi_ref, bi_ref,
                  lam_ref, gv_ref, wsp_ref, bspt_ref, o_ref, ext_ref, hprev_ref, *, ts, d_lru, heads, groups, chunk):
    s = pl.program_id(1)

    @pl.when(s == 0)
    def _():
        ext_ref[0:SUBLANES, :] = jnp.zeros((SUBLANES, d_lru), _F32)
        hprev_ref[...] = jnp.zeros_like(hprev_ref)

    x = x_ref[...]
    shift = mod_ref[0:1, :]
    scale = mod_ref[1:2, :]
    h = _rms(x, gmix_ref[...]) * (1.0 + scale) + shift
    hb = h.astype(_BF16)

    x_lru = _bdot(hb, win_ref[:, 0:d_lru])
    ext_ref[SUBLANES:SUBLANES + ts, :] = x_lru
    xc = cb_ref[...] + cw_ref[3:4, :] * x_lru
    for k in range(CONV_WIDTH - 1):
        off = SUBLANES - (CONV_WIDTH - 1) + k
        xc = xc + cw_ref[k:k + 1, :] * ext_ref[off:off + ts, :]
    ext_ref[0:SUBLANES, :] = x_lru[ts - SUBLANES:ts, :]

    xcb = xc.astype(_BF16)
    blk = d_lru // heads
    ra, ia = [], []
    for hh in range(heads):
        xh = xcb[:, hh * blk:(hh + 1) * blk]
        ra.append(_bdot(xh, wa_ref[hh]))
        ia.append(_bdot(xh, wi_ref[hh]))
    r = jax.nn.sigmoid(jnp.concatenate(ra, axis=1) + ba_ref[...])
    ig = jax.nn.sigmoid(jnp.concatenate(ia, axis=1) + bi_ref[...])
    log_a = (-LRU_C) * r * _softplus(-lam_ref[...])
    a = jnp.exp(log_a)
    bv = jnp.sqrt(-jnp.tanh(log_a) * (a * a + 1.0)) * (ig * xc)

    row = lax.broadcasted_iota(_I32, (ts, d_lru), 0)
    dist = 1
    while dist < ts:
        a_s = pltpu.roll(a, dist, 0)
        b_s = pltpu.roll(bv, dist, 0)
        m = row >= dist
        bv = jnp.where(m, a * b_s + bv, bv)
        a = jnp.where(m, a * a_s, a)
        dist *= 2
    hs = bv + a * hprev_ref[0:1, :]
    hprev_ref[0:1, :] = hs[ts - 1:ts, :]

    y_gate = _bdot(hb, win_ref[:, d_lru:2 * d_lru])
    o_ref[:, 0:d_lru] = (hs * _gelu(y_gate)).astype(_BF16)

    d_sgu = gv_ref.shape[1]
    gd = d_sgu // groups
    u = _gelu(_bdot(hb, win_ref[:, 2 * d_lru:2 * d_lru + d_sgu]))
    v = _gelu(_bdot(hb, win_ref[:, 2 * d_lru + d_sgu:2 * d_lru + 2 * d_sgu]))
    vnb = _rms(v, gv_ref[...]).astype(_BF16)
    tri = lax.broadcasted_iota(_I32, (chunk, chunk), 0) >= lax.broadcasted_iota(_I32, (chunk, chunk), 1)
    for g in range(groups):
        wm = jnp.where(tri, wsp_ref[g], 0.0).astype(_BF16)
        bcol = bspt_ref[:, g:g + 1]
        for n in range(ts // chunk):
            sg = _bdot(wm, vnb[n * chunk:(n + 1) * chunk, g * gd:(g + 1) * gd]) + bcol
            ug = u[n * chunk:(n + 1) * chunk, g * gd:(g + 1) * gd]
            o_ref[n * chunk:(n + 1) * chunk, d_lru + g * gd:d_lru + (g + 1) * gd] = (ug * sg).astype(_BF16)


def _mixer_call(x, mod3, g_mix, w_in_b, conv_w, conv_b, w_a_b, b_a, w_i_b, b_i, lam, g_v, w_sp, b_sp_t, ts):
    bsz, seq, d = x.shape
    d_lru = conv_w.shape[1]
    d_sgu = g_v.shape[1]
    heads = w_a_b.shape[0]
    groups, chunk, _ = w_sp.shape
    d_mix = d_lru + d_sgu
    const = lambda shape: pl.BlockSpec(shape, lambda b, s: (0,) * len(shape), pipeline_mode=pl.Buffered(1))
    kern = functools.partial(_mixer_kernel, ts=ts, d_lru=d_lru, heads=heads, groups=groups, chunk=chunk)
    return pl.pallas_call(
        kern,
        grid=(bsz, seq // ts),
        in_specs=[
            pl.BlockSpec((None, ts, d), lambda b, s: (b, s, 0)),
            pl.BlockSpec((None, 6, d), lambda b, s: (b, 0, 0)),
            const((1, d)),
            const(w_in_b.shape),
            const(conv_w.shape),
            const((1, d_lru)),
            const(w_a_b.shape),
            const((1, d_lru)),
            const(w_i_b.shape),
            const((1, d_lru)),
            const((1, d_lru)),
            const((1, d_sgu)),
            const(w_sp.shape),
            const(b_sp_t.shape),
        ],
        out_specs=pl.BlockSpec((None, ts, d_mix), lambda b, s: (b, s, 0)),
        out_shape=jax.ShapeDtypeStruct((bsz, seq, d_mix), _BF16),
        scratch_shapes=[pltpu.VMEM((ts + SUBLANES, d_lru), _F32), pltpu.VMEM((SUBLANES, d_lru), _F32)],
        compiler_params=pltpu.CompilerParams(
            dimension_semantics=("arbitrary", "arbitrary"), vmem_limit_bytes=56 * 1024 * 1024),
    )(x, mod3, g_mix, w_in_b, conv_w, conv_b, w_a_b, b_a, w_i_b, b_i, lam, g_v, w_sp, b_sp_t)


def _topk_cols(vals, pos, ids, k, tick=None):
    n, t = vals.shape
    kio = lax.broadcasted_iota(_I32, (k, t), 0)
    out_v = jnp.zeros((k, t), _F32)
    out_i = jnp.zeros((k, t), _F32)
    for j in range(k):
        if tick is not None:
            tick()
        m = jnp.max(vals, axis=0, keepdims=True)
        p = jnp.min(jnp.where(vals == m, pos, _POS_INF), axis=0, keepdims=True)
        hit = pos == p
        e = p if ids is None else jnp.max(jnp.where(hit, ids, -1.0), axis=0, keepdims=True)
        out_v = jnp.where(kio == j, m, out_v)
        out_i = jnp.where(kio == j, e, out_i)
        vals = jnp.where(hit, _NEG_INF, vals)
    return out_v, out_i


def _pair_candidates(v1, i1, v2, i2, n_keys):
    k, t = v1.shape
    half = k // 2
    jrow = lax.broadcasted_iota(_I32, (half, t), 0)
    vals = [v1[0:1, :] + v2]
    pos = [lax.broadcasted_iota(_I32, (k, t), 0).astype(_F32)]
    ids = [i1[0:1, :] * n_keys + i2]
    for i in range(1, half):
        keep = jrow < k // (i + 1)
        vals.append(jnp.where(keep, v1[i:i + 1, :] + v2[0:half, :], _NEG_INF))
        pos.append((jrow + i * k).astype(_F32))
        ids.append(i1[i:i + 1, :] * n_keys + i2[0:half, :])
    vals.append(v1[half:k, :] + v2[0:1, :])
    pos.append(((jrow + half) * k).astype(_F32))
    ids.append(i1[half:k, :] * n_keys + i2[0:1, :])
    return jnp.concatenate(vals, axis=0), jnp.concatenate(pos, axis=0), jnp.concatenate(ids, axis=0)


def _route_kernel(y_ref, x_ref, mod_ref, wout_ref, gffn_ref, wq_ref, keys_ref,
                  x1_ref, h2_ref, sc_ref, *, heads):
    x1 = x_ref[...] + mod_ref[2:3, :] * _bdot(y_ref[...], wout_ref[...])
    x1_ref[...] = x1
    h2 = _rms(x1, gffn_ref[...]) * (1.0 + mod_ref[4:5, :]) + mod_ref[3:4, :]
    h2_ref[...] = h2
    q = _bdot(h2.astype(_BF16), wq_ref[...]).astype(_BF16)
    dk = keys_ref.shape[-1]
    nt = (((1,), (1,)), ((), ()))
    for c in range(2 * heads):
        sc_ref[c] = lax.dot_general(keys_ref[c // 2, c % 2], q[:, c * dk:(c + 1) * dk], nt,
                                    preferred_element_type=_F32)


def _route_call(y, x, mod3, w_out_b, g_ffn, w_q_b, keys_b, tq):
    bsz, seq, d = x.shape
    d_mix = y.shape[-1]
    heads, _, n_keys, dk = keys_b.shape
    nq = seq // tq
    const = lambda shape: pl.BlockSpec(shape, lambda b, s: (0,) * len(shape), pipeline_mode=pl.Buffered(1))
    kern = functools.partial(_route_kernel, heads=heads)
    return pl.pallas_call(
        kern,
        grid=(bsz, nq),
        in_specs=[
            pl.BlockSpec((None, tq, d_mix), lambda b, s: (b, s, 0)),
            pl.BlockSpec((None, tq, d), lambda b, s: (b, s, 0)),
            pl.BlockSpec((None, 6, d), lambda b, s: (b, 0, 0)),
            const(w_out_b.shape),
            const((1, d)),
            const(w_q_b.shape),
            const(keys_b.shape),
        ],
        out_specs=[
            pl.BlockSpec((None, tq, d), lambda b, s: (b, s, 0)),
            pl.BlockSpec((None, tq, d), lambda b, s: (b, s, 0)),
            pl.BlockSpec((2 * heads, n_keys, tq), lambda b, s: (0, 0, b * nq + s)),
        ],
        out_shape=[
            jax.ShapeDtypeStruct((bsz, seq, d), _F32),
            jax.ShapeDtypeStruct((bsz, seq, d), _F32),
            jax.ShapeDtypeStruct((2 * heads, n_keys, bsz * seq), _F32),
        ],
        compiler_params=pltpu.CompilerParams(
            dimension_semantics=("arbitrary", "arbitrary"), vmem_limit_bytes=56 * 1024 * 1024),
    )(y, x, mod3, w_out_b, g_ffn, w_q_b, keys_b)


def _pack_kernel(u_ref, v_ref, o_ref):
    o_ref[...] = pltpu.pack_elementwise([u_ref[...], v_ref[...]], packed_dtype=_BF16)


def _pack_call(expert_u, expert_v, rows):
    n, d = expert_u.shape
    spec = pl.BlockSpec((rows, d), lambda i: (i, 0))
    return pl.pallas_call(
        _pack_kernel,
        grid=(n // rows,),
        in_specs=[spec, spec],
        out_specs=spec,
        out_shape=jax.ShapeDtypeStruct((n, d), jnp.uint32),
        compiler_params=pltpu.CompilerParams(
            dimension_semantics=("arbitrary",), vmem_limit_bytes=40 * 1024 * 1024),
    )(expert_u, expert_v)


def _peer_kernel(sc_ref, h_ref, x1_ref, mod_ref, gfin_ref, tab_ref, tabw_ref, o_ref,
                 buf0_ref, buf1_ref, sem_ref, zero_ref, idxv_ref, gt_ref, gs_ref, idxs_ref, ssem_ref,
                 *, tb, nsel, d, nsb, n_keys):
    s = pl.program_id(0)
    k = pl.program_id(1)
    steps = pl.num_programs(1)
    nct = d // LANES
    tok_tiles = (nsel // SUBLANES) * nct
    bufs = (buf0_ref, buf1_ref)
    sb_tokens = sc_ref.shape[-1]
    nsteps = sb_tokens // tb

    def row_copy(row, sl, t, e):
        first = t * tok_tiles + (e // SUBLANES) * nct
        return pltpu.make_async_copy(
            tab_ref.at[row], bufs[sl].at[pl.ds(first, nct), e % SUBLANES, :], sem_ref.at[sl, t])

    def token_wait(sl, t):
        pltpu.make_async_copy(
            tabw_ref.at[pl.ds(0, tok_tiles)], bufs[sl].at[pl.ds(t * tok_tiles, tok_tiles)], sem_ref.at[sl, t]).wait()

    key_pos = lax.broadcasted_iota(_I32, (n_keys, sb_tokens), 0).astype(_F32)

    def route_head(tick):
        tops = [_topk_cols(sc_ref[p], key_pos, None, PEER_TOPK, tick) for p in range(2)]
        (v1, i1), (v2, i2) = tops
        cand, cpos, cids = _pair_candidates(v1, i1, v2, i2, float(n_keys))
        tv, ti = _topk_cols(cand, cpos, cids, PEER_TOPK, tick)
        ex = jnp.exp(tv - tv[0:1, :])
        r = pl.multiple_of(k * PEER_TOPK, PEER_TOPK)
        idxv_ref[pl.ds(r, PEER_TOPK), :] = ti.astype(_I32)
        gt_ref[pl.ds(r, PEER_TOPK), :] = ex / jnp.sum(ex, axis=0, keepdims=True)

    def publish():
        @pl.when(k == steps - 1)
        def _():
            slot = s % 3
            gs_ref[slot] = gt_ref[...].T
            cp = pltpu.make_async_copy(idxv_ref, idxs_ref.at[slot], ssem_ref)
            cp.start()
            cp.wait()

    ngrp = nsel // SUBLANES
    rows8 = lax.broadcasted_iota(_I32, (SUBLANES, 2 * nsel), 0)
    lane = lax.broadcasted_iota(_I32, (nsel, LANES), 1)
    spread = (lax.broadcasted_iota(_I32, (nsel, 2 * nsel), 1)
              == 2 * lax.broadcasted_iota(_I32, (nsel, 2 * nsel), 0) + 1).astype(_BF16)
    final = jnp.logical_and(s == nsb + 1, k == steps - 1)
    nxt = jnp.where(final, k, k + 1)
    nslot = (s - 2 + nxt // nsteps) % 3
    ntok0 = (nxt % nsteps) * tb

    def next_row(t, e):
        return idxs_ref[nslot, e, ntok0 + t]

    def token_matrix(sl, t):
        w = bufs[sl][t * tok_tiles:(t + 1) * tok_tiles]
        rows = [jnp.concatenate([w[g * nct + c] for c in range(nct)], axis=1) for g in range(ngrp)]
        return pltpu.bitcast(jnp.concatenate(rows, axis=0), _BF16)

    def u_dots(sl, t):
        acc = [jnp.zeros((SUBLANES, LANES), _F32) for _ in range(ngrp)]
        for c in range(nct):
            xc = h_ref[t:t + 1, c * LANES:(c + 1) * LANES] + zero_ref[...]
            for g in range(ngrp):
                w = bufs[sl][t * tok_tiles + g * nct + c]
                u = pltpu.unpack_elementwise(w, index=0, packed_dtype=_BF16, unpacked_dtype=_F32)
                acc[g] = acc[g] + u * xc
        return jnp.sum(jnp.concatenate(acc, axis=0), axis=1, keepdims=True)

    def evaluate(sl):
        nsl = 1 - sl
        ngroups = tb // SUBLANES
        copies = [(t, e) for t in range(tb) for e in range(nsel)]
        nticks = 3 * PEER_TOPK
        per_tick = len(copies) // (8 * nticks)
        issued = [0]

        def issue(n):
            for t, e in copies[issued[0]:issued[0] + n]:
                row_copy(next_row(t, e), nsl, t, e).start(priority=e % 2)
            issued[0] += n

        route_head(lambda: issue(per_tick))
        publish()
        nregions = (ngroups + 1) * SUBLANES
        share = -(-(len(copies) - issued[0]) // nregions)
        gslot = (s - 2) % 3
        act2 = None
        for p in range(ngroups + 1):
            r0, v0 = p * SUBLANES, (p - 1) * SUBLANES
            zcols = jnp.zeros((nsel, LANES), _F32)
            out = jnp.zeros((SUBLANES, d), _F32)
            for j in range(SUBLANES):
                if p < ngroups:
                    token_wait(sl, r0 + j)
                issue(share)
                if p > 0:
                    aj = jnp.where(rows8 == j, act2, 0.0).astype(_BF16)
                    out = out + _bdot(aj, token_matrix(sl, v0 + j))
                if p < ngroups:
                    zcols = jnp.where(lane == j, u_dots(sl, r0 + j), zcols)
            if p > 0:
                x2 = x1_ref[v0:v0 + SUBLANES, :] + mod_ref[5:6, :] * out
                o_ref[v0:v0 + SUBLANES, :] = _rms(x2, gfin_ref[...])
            if p < ngroups:
                z = zcols.T[0:SUBLANES, :]
                g_rows = gs_ref[gslot, pl.ds(pl.multiple_of(k * tb + r0, SUBLANES), SUBLANES), :]
                act = (_gelu(z) * g_rows).astype(_BF16)
                act2 = _bdot(act, spread)

    @pl.when(s < 2)
    def _():
        @pl.when(jnp.logical_and(s == 0, k == 0))
        def _():
            zero_ref[...] = jnp.zeros_like(zero_ref)

        route_head(None)
        publish()

        @pl.when(jnp.logical_and(s == 1, k == steps - 1))
        def _():
            def body(t, c):
                for e in range(nsel):
                    row_copy(idxs_ref[0, e, t], 0, t, e).start(priority=e % 2)
                return c
            lax.fori_loop(0, tb, body, 0)

    for sl in range(2):
        @pl.when(jnp.logical_and(s >= 2, k % 2 == sl))
        def _(sl=sl):
            evaluate(sl)

    @pl.when(final)
    def _():
        for t in range(tb):
            token_wait(nsteps % 2, t)


def _peer_call(sc, h2, x1, mod3, g_final, tab, tb, seq, nsel):
    nhh, n_keys, t_total = sc.shape
    d = h2.shape[-1]
    sb_tokens = LANES
    nsteps = sb_tokens // tb
    assert nhh == 2 * nsteps, "one routing head per evaluation step"
    nsb = t_total // sb_tokens
    nct = d // LANES
    n_exp = tab.shape[0]
    tab3 = tab.reshape(n_exp, nct, LANES)
    tabw = tab.reshape(n_exp * nct // SUBLANES, SUBLANES, LANES)
    slot_tiles = tb * (nsel // SUBLANES) * nct
    blk = lambda s, k: jnp.maximum((s - 2) * nsteps + k, 0)
    kern = functools.partial(_peer_kernel, tb=tb, nsel=nsel, d=d, nsb=nsb, n_keys=n_keys)
    return pl.pallas_call(
        kern,
        grid=(nsb + 2, nsteps),
        in_specs=[
            pl.BlockSpec((2, n_keys, sb_tokens), lambda s, k: (k, 0, jnp.minimum(s, nsb - 1))),
            pl.BlockSpec((tb, d), lambda s, k: (blk(s, k), 0)),
            pl.BlockSpec((tb, d), lambda s, k: (blk(s, k), 0)),
            pl.BlockSpec((None, 6, d), lambda s, k: ((blk(s, k) * tb) // seq, 0, 0)),
            pl.BlockSpec((1, d), lambda s, k: (0, 0)),
            pl.BlockSpec(memory_space=pl.ANY),
            pl.BlockSpec(memory_space=pl.ANY),
        ],
        out_specs=pl.BlockSpec((tb, d), lambda s, k: (blk(s, k), 0)),
        out_shape=jax.ShapeDtypeStruct((t_total, d), _F32),
        scratch_shapes=[
            pltpu.VMEM((slot_tiles, SUBLANES, LANES), jnp.uint32),
            pltpu.VMEM((slot_tiles, SUBLANES, LANES), jnp.uint32),
            pltpu.SemaphoreType.DMA((2, tb)),
            pltpu.VMEM((SUBLANES, LANES), _F32),
            pltpu.VMEM((nsel, sb_tokens), _I32),
            pltpu.VMEM((nsel, sb_tokens), _F32),
            pltpu.VMEM((3, sb_tokens, nsel), _F32),
            pltpu.SMEM((3, nsel, sb_tokens), _I32),
            pltpu.SemaphoreType.DMA(()),
        ],
        compiler_params=pltpu.CompilerParams(
            dimension_semantics=("arbitrary", "arbitrary"), vmem_limit_bytes=56 * 1024 * 1024),
    )(sc, h2, x1, mod3, g_final, tab3, tabw)


def _layer(x, mod3, g_norm_mix, w_in, conv_w, conv_b, w_gate_a, b_gate_a, w_gate_i, b_gate_i, lru_lambda, g_v,
           w_spatial, b_spatial, w_out, g_norm_ffn, w_query, sub_keys, expert_u, expert_v, g_out, ts, tq, tb):
    bsz, seq, d = x.shape
    d_lru = conv_w.shape[-1]
    d_sgu = g_v.shape[-1]
    y = _mixer_call(
        x, mod3, g_norm_mix.reshape(1, d), w_in.astype(_BF16), conv_w, conv_b.reshape(1, d_lru),
        w_gate_a.astype(_BF16), b_gate_a.reshape(1, d_lru), w_gate_i.astype(_BF16), b_gate_i.reshape(1, d_lru),
        lru_lambda.reshape(1, d_lru), g_v.reshape(1, d_sgu), w_spatial, b_spatial.T, ts)
    x1, h2, sc = _route_call(
        y, x, mod3, w_out.astype(_BF16), g_norm_ffn.reshape(1, d), w_query.astype(_BF16), sub_keys.astype(_BF16), tq)
    t_total = bsz * seq
    nsel = sub_keys.shape[0] * PEER_TOPK
    tab = _pack_call(expert_u, expert_v, min(256, expert_u.shape[0]))
    out = _peer_call(sc, h2.reshape(t_total, d), x1.reshape(t_total, d), mod3, g_out.reshape(1, d), tab, tb, seq, nsel)
    return out.reshape(bsz, seq, d)


def kernel(x, c, w_ada, b_ada, g_norm_mix, w_in, conv_w, conv_b, w_gate_a, b_gate_a, w_gate_i, b_gate_i,
           lru_lambda, g_v, w_spatial, b_spatial, w_out, g_norm_ffn, w_query, sub_keys, expert_u, expert_v, g_final):
    depth = w_ada.shape[0]
    assert depth == 1, "the final RMSNorm is fused into the layer's last call"
    bsz, seq, d = x.shape
    ts = min(256, seq)
    tq = min(256, seq)
    tb = 16
    n_mod = w_ada.shape[-1]
    mod = _ada_call(c, w_ada[0], b_ada[0], 1024 if n_mod % 1024 == 0 else n_mod)
    mod3 = mod.reshape(bsz, 6, d)
    return _layer(x, mod3, g_norm_mix[0], w_in[0], conv_w[0], conv_b[0], w_gate_a[0], b_gate_a[0], w_gate_i[0],
                  b_gate_i[0], lru_lambda[0], g_v[0], w_spatial[0], b_spatial[0], w_out[0], g_norm_ffn[0],
                  w_query[0], sub_keys[0], expert_u[0], expert_v[0], g_final, ts, tq, tb)
```

```python
import functools

import jax
import jax.numpy as jnp
from jax import lax
from jax.experimental import pallas as pl
from jax.experimental.pallas import tpu as pltpu

_F32 = jnp.float32
_BF16 = jnp.bfloat16
_I32 = jnp.int32

EPS = 1e-6
LRU_C = 8.0
CONV_WIDTH = 4
PEER_TOPK = 16
LANES = 128
SUBLANES = 8
_NEG_INF = float("-inf")
_POS_INF = float("inf")


def _gelu(x):
    return 0.5 * x * (1.0 + jnp.tanh(0.7978845608028654 * (x + 0.044715 * (x * x * x))))


def _softplus(y):
    return jnp.maximum(y, 0.0) + jnp.log1p(jnp.exp(-jnp.abs(y)))


def _rms(x, g):
    return x * lax.rsqrt(jnp.mean(x * x, axis=-1, keepdims=True) + EPS) * g


def _bdot(a, b):
    return jnp.dot(a, b, preferred_element_type=_F32)


def _ada_kernel(c_ref, w_ref, b_ref, o_ref):
    c = c_ref[...]
    a = c * jax.nn.sigmoid(c)
    o_ref[...] = _bdot(a.astype(_BF16), w_ref[...].astype(_BF16)) + b_ref[...]


def _ada_call(c, w, b, tn):
    bsz, d = c.shape
    n = w.shape[1]
    return pl.pallas_call(
        _ada_kernel,
        grid=(n // tn,),
        in_specs=[
            pl.BlockSpec((bsz, d), lambda j: (0, 0)),
            pl.BlockSpec((d, tn), lambda j: (0, j)),
            pl.BlockSpec((1, tn), lambda j: (0, j)),
        ],
        out_specs=pl.BlockSpec((bsz, tn), lambda j: (0, j)),
        out_shape=jax.ShapeDtypeStruct((bsz, n), _F32),
        compiler_params=pltpu.CompilerParams(
            dimension_semantics=("arbitrary",), vmem_limit_bytes=40 * 1024 * 1024),
    )(c, w, b.reshape(1, n))


def _mixer_kernel(x_ref, mod_ref, gmix_ref, win_ref, cw_ref, cb_ref, wa_ref, ba_ref, wi_ref, bi_ref,
                  lam_ref, gv_ref, wsp_ref, bspt_ref, o_ref, ext_ref, hprev_ref, *, ts, d_lru, heads, groups, chunk):
    s = pl.program_id(1)

    @pl.when(s == 0)
    def _():
        ext_ref[0:SUBLANES, :] = jnp.zeros((SUBLANES, d_lru), _F32)
        hprev_ref[...] = jnp.zeros_like(hprev_ref)

    x = x_ref[...]
    shift = mod_ref[0:1, :]
    scale = mod_ref[1:2, :]
    h = _rms(x, gmix_ref[...]) * (1.0 + scale) + shift
    hb = h.astype(_BF16)

    x_lru = _bdot(hb, win_ref[:, 0:d_lru])
    ext_ref[SUBLANES:SUBLANES + ts, :] = x_lru
    xc = cb_ref[...] + cw_ref[3:4, :] * x_lru
    for k in range(CONV_WIDTH - 1):
        off = SUBLANES - (CONV_WIDTH - 1) + k
        xc = xc + cw_ref[k:k + 1, :] * ext_ref[off:off + ts, :]
    ext_ref[0:SUBLANES, :] = x_lru[ts - SUBLANES:ts, :]

    xcb = xc.astype(_BF16)
    blk = d_lru // heads
    ra, ia = [], []
    for hh in range(heads):
        xh = xcb[:, hh * blk:(hh + 1) * blk]
        ra.append(_bdot(xh, wa_ref[hh]))
        ia.append(_bdot(xh, wi_ref[hh]))
    r = jax.nn.sigmoid(jnp.concatenate(ra, axis=1) + ba_ref[...])
    ig = jax.nn.sigmoid(jnp.concatenate(ia, axis=1) + bi_ref[...])
    log_a = (-LRU_C) * r * _softplus(-lam_ref[...])
    a = jnp.exp(log_a)
    bv = jnp.sqrt(-jnp.tanh(log_a) * (a * a + 1.0)) * (ig * xc)

    row = lax.broadcasted_iota(_I32, (ts, d_lru), 0)
    dist = 1
    while dist < ts:
        a_s = pltpu.roll(a, dist, 0)
        b_s = pltpu.roll(bv, dist, 0)
        m = row >= dist
        bv = jnp.where(m, a * b_s + bv, bv)
        a = jnp.where(m, a * a_s, a)
        dist *= 2
    hs = bv + a * hprev_ref[0:1, :]
    hprev_ref[0:1, :] = hs[ts - 1:ts, :]

    y_gate = _bdot(hb, win_ref[:, d_lru:2 * d_lru])
    o_ref[:, 0:d_lru] = (hs * _gelu(y_gate)).astype(_BF16)

    d_sgu = gv_ref.shape[1]
    gd = d_sgu // groups
    u = _gelu(_bdot(hb, win_ref[:, 2 * d_lru:2 * d_lru + d_sgu]))
    v = _gelu(_bdot(hb, win_ref[:, 2 * d_lru + d_sgu:2 * d_lru + 2 * d_sgu]))
    vnb = _rms(v, gv_ref[...]).astype(_BF16)
    tri = lax.broadcasted_iota(_I32, (chunk, chunk), 0) >= lax.broadcasted_iota(_I32, (chunk, chunk), 1)
    for g in range(groups):
        wm = jnp.where(tri, wsp_ref[g], 0.0).astype(_BF16)
        bcol = bspt_ref[:, g:g + 1]
        for n in range(ts // chunk):
            sg = _bdot(wm, vnb[n * chunk:(n + 1) * chunk, g * gd:(g + 1) * gd]) + bcol
            ug = u[n * chunk:(n + 1) * chunk, g * gd:(g + 1) * gd]
            o_ref[n * chunk:(n + 1) * chunk, d_lru + g * gd:d_lru + (g + 1) * gd] = (ug * sg).astype(_BF16)


def _mixer_call(x, mod3, g_mix, w_in_b, conv_w, conv_b, w_a_b, b_a, w_i_b, b_i, lam, g_v, w_sp, b_sp_t, ts):
    bsz, seq, d = x.shape
    d_lru = conv_w.shape[1]
    d_sgu = g_v.shape[1]
    heads = w_a_b.shape[0]
    groups, chunk, _ = w_sp.shape
    d_mix = d_lru + d_sgu
    const = lambda shape: pl.BlockSpec(shape, lambda b, s: (0,) * len(shape), pipeline_mode=pl.Buffered(1))
    kern = functools.partial(_mixer_kernel, ts=ts, d_lru=d_lru, heads=heads, groups=groups, chunk=chunk)
    return pl.pallas_call(
        kern,
        grid=(bsz, seq // ts),
        in_specs=[
            pl.BlockSpec((None, ts, d), lambda b, s: (b, s, 0)),
            pl.BlockSpec((None, 6, d), lambda b, s: (b, 0, 0)),
            const((1, d)),
            const(w_in_b.shape),
            const(conv_w.shape),
            const((1, d_lru)),
            const(w_a_b.shape),
            const((1, d_lru)),
            const(w_i_b.shape),
            const((1, d_lru)),
            const((1, d_lru)),
            const((1, d_sgu)),
            const(w_sp.shape),
            const(b_sp_t.shape),
        ],
        out_specs=pl.BlockSpec((None, ts, d_mix), lambda b, s: (b, s, 0)),
        out_shape=jax.ShapeDtypeStruct((bsz, seq, d_mix), _BF16),
        scratch_shapes=[pltpu.VMEM((ts + SUBLANES, d_lru), _F32), pltpu.VMEM((SUBLANES, d_lru), _F32)],
        compiler_params=pltpu.CompilerParams(
            dimension_semantics=("arbitrary", "arbitrary"), vmem_limit_bytes=56 * 1024 * 1024),
    )(x, mod3, g_mix, w_in_b, conv_w, conv_b, w_a_b, b_a, w_i_b, b_i, lam, g_v, w_sp, b_sp_t)


def _topk_cols(vals, pos, ids, k, tick=None):
    n, t = vals.shape
    kio = lax.broadcasted_iota(_I32, (k, t), 0)
    out_v = jnp.zeros((k, t), _F32)
    out_i = jnp.zeros((k, t), _F32)
    for j in range(k):
        if tick is not None:
            tick()
        m = jnp.max(vals, axis=0, keepdims=True)
        p = jnp.min(jnp.where(vals == m, pos, _POS_INF), axis=0, keepdims=True)
        hit = pos == p
        e = p if ids is None else jnp.max(jnp.where(hit, ids, -1.0), axis=0, keepdims=True)
        out_v = jnp.where(kio == j, m, out_v)
        out_i = jnp.where(kio == j, e, out_i)
        vals = jnp.where(hit, _NEG_INF, vals)
    return out_v, out_i


def _pair_candidates(v1, i1, v2, i2, n_keys):
    k, t = v1.shape
    half = k // 2
    jrow = lax.broadcasted_iota(_I32, (half, t), 0)
    vals = [v1[0:1, :] + v2]
    pos = [lax.broadcasted_iota(_I32, (k, t), 0).astype(_F32)]
    ids = [i1[0:1, :] * n_keys + i2]
    for i in range(1, half):
        keep = jrow < k // (i + 1)
        vals.append(jnp.where(keep, v1[i:i + 1, :] + v2[0:half, :], _NEG_INF))
        pos.append((jrow + i * k).astype(_F32))
        ids.append(i1[i:i + 1, :] * n_keys + i2[0:half, :])
    vals.append(v1[half:k, :] + v2[0:1, :])
    pos.append(((jrow + half) * k).astype(_F32))
    ids.append(i1[half:k, :] * n_keys + i2[0:1, :])
    return jnp.concatenate(vals, axis=0), jnp.concatenate(pos, axis=0), jnp.concatenate(ids, axis=0)


def _route_kernel(y_ref, x_ref, mod_ref, wout_ref, gffn_ref, wq_ref, keys_ref,
                  x1_ref, h2_ref, sc_ref, *, heads):
    x1 = x_ref[...] + mod_ref[2:3, :] * _bdot(y_ref[...], wout_ref[...])
    x1_ref[...] = x1
    h2 = _rms(x1, gffn_ref[...]) * (1.0 + mod_ref[4:5, :]) + mod_ref[3:4, :]
    h2_ref[...] = h2
    q = _bdot(h2.astype(_BF16), wq_ref[...]).astype(_BF16)
    dk = keys_ref.shape[-1]
    nt = (((1,), (1,)), ((), ()))
    for c in range(2 * heads):
        sc_ref[c] = lax.dot_general(keys_ref[c // 2, c % 2], q[:, c * dk:(c + 1) * dk], nt,
                                    preferred_element_type=_F32)


def _route_call(y, x, mod3, w_out_b, g_ffn, w_q_b, keys_b, tq):
    bsz, seq, d = x.shape
    d_mix = y.shape[-1]
    heads, _, n_keys, dk = keys_b.shape
    nq = seq // tq
    const = lambda shape: pl.BlockSpec(shape, lambda b, s: (0,) * len(shape), pipeline_mode=pl.Buffered(1))
    kern = functools.partial(_route_kernel, heads=heads)
    return pl.pallas_call(
        kern,
        grid=(bsz, nq),
        in_specs=[
            pl.BlockSpec((None, tq, d_mix), lambda b, s: (b, s, 0)),
            pl.BlockSpec((None, tq, d), lambda b, s: (b, s, 0)),
            pl.BlockSpec((None, 6, d), lambda b, s: (b, 0, 0)),
            const(w_out_b.shape),
            const((1, d)),
            const(w_q_b.shape),
            const(keys_b.shape),
        ],
        out_specs=[
            pl.BlockSpec((None, tq, d), lambda b, s: (b, s, 0)),
            pl.BlockSpec((None, tq, d), lambda b, s: (b, s, 0)),
            pl.BlockSpec((2 * heads, n_keys, tq), lambda b, s: (0, 0, b * nq + s)),
        ],
        out_shape=[
            jax.ShapeDtypeStruct((bsz, seq, d), _F32),
            jax.ShapeDtypeStruct((bsz, seq, d), _F32),
            jax.ShapeDtypeStruct((2 * heads, n_keys, bsz * seq), _F32),
        ],
        compiler_params=pltpu.CompilerParams(
            dimension_semantics=("arbitrary", "arbitrary"), vmem_limit_bytes=56 * 1024 * 1024),
    )(y, x, mod3, w_out_b, g_ffn, w_q_b, keys_b)


def _pack_kernel(u_ref, v_ref, o_ref):
    w = pltpu.pack_elementwise([u_ref[...], v_ref[...]], packed_dtype=_BF16)
    for c in range(o_ref.shape[1]):
        o_ref[:, c, :] = w[:, c * LANES:(c + 1) * LANES]


def _pack_call(expert_u, expert_v, rows):
    n, d = expert_u.shape
    spec = pl.BlockSpec((rows, d), lambda i: (i, 0))
    return pl.pallas_call(
        _pack_kernel,
        grid=(n // rows,),
        in_specs=[spec, spec],
        out_specs=pl.BlockSpec((rows, d // LANES, LANES), lambda i: (i, 0, 0)),
        out_shape=jax.ShapeDtypeStruct((n, d // LANES, LANES), jnp.uint32),
        compiler_params=pltpu.CompilerParams(
            dimension_semantics=("arbitrary",), vmem_limit_bytes=40 * 1024 * 1024),
    )(expert_u, expert_v)


def _peer_kernel(sc_ref, h_ref, x1_ref, mod_ref, gfin_ref, tab_ref, tabw_ref, o_ref,
                 buf0_ref, buf1_ref, sem_ref, zero_ref, idxv_ref, gt_ref, gs_ref, idxs_ref, ssem_ref,
                 *, tb, nsel, d, nsb, n_keys):
    s = pl.program_id(0)
    k = pl.program_id(1)
    steps = pl.num_programs(1)
    nct = d // LANES
    tok_tiles = (nsel // SUBLANES) * nct
    bufs = (buf0_ref, buf1_ref)
    sb_tokens = sc_ref.shape[-1]
    nsteps = sb_tokens // tb

    def row_copy(row, sl, t, e):
        first = t * tok_tiles + (e // SUBLANES) * nct
        return pltpu.make_async_copy(
            tab_ref.at[row], bufs[sl].at[pl.ds(first, nct), e % SUBLANES, :], sem_ref.at[sl, t])

    def token_wait(sl, t):
        pltpu.make_async_copy(
            tabw_ref.at[pl.ds(0, tok_tiles)], bufs[sl].at[pl.ds(t * tok_tiles, tok_tiles)], sem_ref.at[sl, t]).wait()

    key_pos = lax.broadcasted_iota(_I32, (n_keys, sb_tokens), 0).astype(_F32)

    def route_head(tick):
        tops = [_topk_cols(sc_ref[p], key_pos, None, PEER_TOPK, tick) for p in range(2)]
        (v1, i1), (v2, i2) = tops
        cand, cpos, cids = _pair_candidates(v1, i1, v2, i2, float(n_keys))
        tv, ti = _topk_cols(cand, cpos, cids, PEER_TOPK, tick)
        ex = jnp.exp(tv - tv[0:1, :])
        r = pl.multiple_of(k * PEER_TOPK, PEER_TOPK)
        idxv_ref[pl.ds(r, PEER_TOPK), :] = ti.astype(_I32)
        gt_ref[pl.ds(r, PEER_TOPK), :] = ex / jnp.sum(ex, axis=0, keepdims=True)

    def ids_to_smem(slot):
        return pltpu.make_async_copy(idxv_ref, idxs_ref.at[slot], ssem_ref)

    def publish():
        @pl.when(k == steps - 1)
        def _():
            gs_ref[s % 3] = gt_ref[...].T
            ids_to_smem(s % 3).start()

    def published():
        @pl.when(jnp.logical_and(k == 0, s > 0))
        def _():
            ids_to_smem((s - 1) % 3).wait()

    ngrp = nsel // SUBLANES
    rows8 = lax.broadcasted_iota(_I32, (SUBLANES, 2 * nsel), 0)
    lane = lax.broadcasted_iota(_I32, (nsel, LANES), 1)
    spread = (lax.broadcasted_iota(_I32, (nsel, 2 * nsel), 1)
              == 2 * lax.broadcasted_iota(_I32, (nsel, 2 * nsel), 0) + 1).astype(_BF16)
    final = jnp.logical_and(s == nsb + 1, k == steps - 1)
    nxt = jnp.where(final, k, k + 1)
    nslot = (s - 2 + nxt // nsteps) % 3
    ntok0 = (nxt % nsteps) * tb

    def next_row(t, e):
        return idxs_ref[nslot, e, ntok0 + t]

    def token_matrix(sl, t):
        w = bufs[sl][t * tok_tiles:(t + 1) * tok_tiles]
        rows = [jnp.concatenate([w[g * nct + c] for c in range(nct)], axis=1) for g in range(ngrp)]
        return pltpu.bitcast(jnp.concatenate(rows, axis=0), _BF16)

    def u_dots(sl, t):
        acc = [jnp.zeros((SUBLANES, LANES), _F32) for _ in range(ngrp)]
        for c in range(nct):
            xc = h_ref[t:t + 1, c * LANES:(c + 1) * LANES] + zero_ref[...]
            for g in range(ngrp):
                w = bufs[sl][t * tok_tiles + g * nct + c]
                u = pltpu.unpack_elementwise(w, index=0, packed_dtype=_BF16, unpacked_dtype=_F32)
                acc[g] = acc[g] + u * xc
        return jnp.sum(jnp.concatenate(acc, axis=0), axis=1, keepdims=True)

    def evaluate(sl):
        nsl = 1 - sl
        ngroups = tb // SUBLANES
        copies = [(t, e) for t in range(tb) for e in range(nsel)]
        nticks = 3 * PEER_TOPK
        per_tick = len(copies) // (8 * nticks)
        issued = [0]

        def issue(n):
            for t, e in copies[issued[0]:issued[0] + n]:
                row_copy(next_row(t, e), nsl, t, e).start(priority=e % 2)
            issued[0] += n

        published()
        route_head(lambda: issue(per_tick))
        publish()
        nregions = (ngroups + 1) * SUBLANES
        share = -(-(len(copies) - issued[0]) // nregions)
        gslot = (s - 2) % 3
        act2 = None
        for p in range(ngroups + 1):
            r0, v0 = p * SUBLANES, (p - 1) * SUBLANES
            zcols = jnp.zeros((nsel, LANES), _F32)
            out = jnp.zeros((SUBLANES, d), _F32)
            for j in range(SUBLANES):
                if p < ngroups:
                    token_wait(sl, r0 + j)
                issue(share)
                if p > 0:
                    aj = jnp.where(rows8 == j, act2, 0.0).astype(_BF16)
                    out = out + _bdot(aj, token_matrix(sl, v0 + j))
                if p < ngroups:
                    zcols = jnp.where(lane == j, u_dots(sl, r0 + j), zcols)
            if p > 0:
                x2 = x1_ref[v0:v0 + SUBLANES, :] + mod_ref[5:6, :] * out
                o_ref[v0:v0 + SUBLANES, :] = _rms(x2, gfin_ref[...])
            if p < ngroups:
                z = zcols.T[0:SUBLANES, :]
                g_rows = gs_ref[gslot, pl.ds(pl.multiple_of(k * tb + r0, SUBLANES), SUBLANES), :]
                act = (_gelu(z) * g_rows).astype(_BF16)
                act2 = _bdot(act, spread)

    @pl.when(s < 2)
    def _():
        @pl.when(jnp.logical_and(s == 0, k == 0))
        def _():
            zero_ref[...] = jnp.zeros_like(zero_ref)

        published()
        route_head(None)
        publish()

        @pl.when(jnp.logical_and(s == 1, k == steps - 1))
        def _():
            def body(t, c):
                for e in range(nsel):
                    row_copy(idxs_ref[0, e, t], 0, t, e).start(priority=e % 2)
                return c
            lax.fori_loop(0, tb, body, 0)

    for sl in range(2):
        @pl.when(jnp.logical_and(s >= 2, k % 2 == sl))
        def _(sl=sl):
            evaluate(sl)

    @pl.when(final)
    def _():
        ids_to_smem(s % 3).wait()
        for t in range(tb):
            token_wait(nsteps % 2, t)


def _peer_call(sc, h2, x1, mod3, g_final, tab3, tb, seq, nsel):
    nhh, n_keys, t_total = sc.shape
    d = h2.shape[-1]
    sb_tokens = LANES
    nsteps = sb_tokens // tb
    assert nhh == 2 * nsteps, "one routing head per evaluation step"
    nsb = t_total // sb_tokens
    nct = d // LANES
    n_exp = tab3.shape[0]
    tabw = tab3.reshape(n_exp * nct // SUBLANES, SUBLANES, LANES)
    slot_tiles = tb * (nsel // SUBLANES) * nct
    blk = lambda s, k: jnp.maximum((s - 2) * nsteps + k, 0)
    kern = functools.partial(_peer_kernel, tb=tb, nsel=nsel, d=d, nsb=nsb, n_keys=n_keys)
    return pl.pallas_call(
        kern,
        grid=(nsb + 2, nsteps),
        in_specs=[
            pl.BlockSpec((2, n_keys, sb_tokens), lambda s, k: (k, 0, jnp.minimum(s, nsb - 1))),
            pl.BlockSpec((tb, d), lambda s, k: (blk(s, k), 0)),
            pl.BlockSpec((tb, d), lambda s, k: (blk(s, k), 0)),
            pl.BlockSpec((None, 6, d), lambda s, k: ((blk(s, k) * tb) // seq, 0, 0)),
            pl.BlockSpec((1, d), lambda s, k: (0, 0)),
            pl.BlockSpec(memory_space=pl.ANY),
            pl.BlockSpec(memory_space=pl.ANY),
        ],
        out_specs=pl.BlockSpec((tb, d), lambda s, k: (blk(s, k), 0)),
        out_shape=jax.ShapeDtypeStruct((t_total, d), _F32),
        scratch_shapes=[
            pltpu.VMEM((slot_tiles, SUBLANES, LANES), jnp.uint32),
            pltpu.VMEM((slot_tiles, SUBLANES, LANES), jnp.uint32),
            pltpu.SemaphoreType.DMA((2, tb)),
            pltpu.VMEM((SUBLANES, LANES), _F32),
            pltpu.VMEM((nsel, sb_tokens), _I32),
            pltpu.VMEM((nsel, sb_tokens), _F32),
            pltpu.VMEM((3, sb_tokens, nsel), _F32),
            pltpu.SMEM((3, nsel, sb_tokens), _I32),
            pltpu.SemaphoreType.DMA(()),
        ],
        compiler_params=pltpu.CompilerParams(
            dimension_semantics=("arbitrary", "arbitrary"), vmem_limit_bytes=56 * 1024 * 1024),
    )(sc, h2, x1, mod3, g_final, tab3, tabw)


def _layer(x, mod3, g_norm_mix, w_in, conv_w, conv_b, w_gate_a, b_gate_a, w_gate_i, b_gate_i, lru_lambda, g_v,
           w_spatial, b_spatial, w_out, g_norm_ffn, w_query, sub_keys, expert_u, expert_v, g_out, ts, tq, tb):
    bsz, seq, d = x.shape
    d_lru = conv_w.shape[-1]
    d_sgu = g_v.shape[-1]
    y = _mixer_call(
        x, mod3, g_norm_mix.reshape(1, d), w_in.astype(_BF16), conv_w, conv_b.reshape(1, d_lru),
        w_gate_a.astype(_BF16), b_gate_a.reshape(1, d_lru), w_gate_i.astype(_BF16), b_gate_i.reshape(1, d_lru),
        lru_lambda.reshape(1, d_lru), g_v.reshape(1, d_sgu), w_spatial, b_spatial.T, ts)
    x1, h2, sc = _route_call(
        y, x, mod3, w_out.astype(_BF16), g_norm_ffn.reshape(1, d), w_query.astype(_BF16), sub_keys.astype(_BF16), tq)
    t_total = bsz * seq
    nsel = sub_keys.shape[0] * PEER_TOPK
    tab = _pack_call(expert_u, expert_v, min(256, expert_u.shape[0]))
    out = _peer_call(sc, h2.reshape(t_total, d), x1.reshape(t_total, d), mod3, g_out.reshape(1, d), tab, tb, seq, nsel)
    return out.reshape(bsz, seq, d)


def kernel(x, c, w_ada, b_ada, g_norm_mix, w_in, conv_w, conv_b, w_gate_a, b_gate_a, w_gate_i, b_gate_i,
           lru_lambda, g_v, w_spatial, b_spatial, w_out, g_norm_ffn, w_query, sub_keys, expert_u, expert_v, g_final):
    depth = w_ada.shape[0]
    assert depth == 1, "the final RMSNorm is fused into the layer's last call"
    bsz, seq, d = x.shape
    ts = min(256, seq)
    tq = min(256, seq)
    tb = 16
    n_mod = w_ada.shape[-1]
    mod = _ada_call(c, w_ada[0], b_ada[0], 1024 if n_mod % 1024 == 0 else n_mod)
    mod3 = mod.reshape(bsz, 6, d)
    return _layer(x, mod3, g_norm_mix[0], w_in[0], conv_w[0], conv_b[0], w_gate_a[0], b_gate_a[0], w_gate_i[0],
                  b_gate_i[0], lru_lambda[0], g_v[0], w_spatial[0], b_spatial[0], w_out[0], g_norm_ffn[0],
                  w_query[0], sub_keys[0], expert_u[0], expert_v[0], g_final, ts, tq, tb)
```

```python
import functools

import jax
import jax.numpy as jnp
from jax import lax
from jax.experimental import pallas as pl
from jax.experimental.pallas import tpu as pltpu

_F32 = jnp.float32
_BF16 = jnp.bfloat16
_I32 = jnp.int32

EPS = 1e-6
LRU_C = 8.0
CONV_WIDTH = 4
PEER_TOPK = 16
LANES = 128
SUBLANES = 8
_NEG_INF = float("-inf")
_POS_INF = float("inf")


def _gelu(x):
    return 0.5 * x * (1.0 + jnp.tanh(0.7978845608028654 * (x + 0.044715 * (x * x * x))))


def _softplus(y):
    return jnp.maximum(y, 0.0) + jnp.log1p(jnp.exp(-jnp.abs(y)))


def _rms(x, g):
    return x * lax.rsqrt(jnp.mean(x * x, axis=-1, keepdims=True) + EPS) * g


def _bdot(a, b):
    return jnp.dot(a, b, preferred_element_type=_F32)


def _ada_kernel(c_ref, w_ref, b_ref, o_ref):
    c = c_ref[...]
    a = c * jax.nn.sigmoid(c)
    o_ref[...] = _bdot(a.astype(_BF16), w_ref[...].astype(_BF16)) + b_ref[...]


def _ada_call(c, w, b, tn):
    bsz, d = c.shape
    n = w.shape[1]
    return pl.pallas_call(
        _ada_kernel,
        grid=(n // tn,),
        in_specs=[
            pl.BlockSpec((bsz, d), lambda j: (0, 0)),
            pl.BlockSpec((d, tn), lambda j: (0, j)),
            pl.BlockSpec((1, tn), lambda j: (0, j)),
        ],
        out_specs=pl.BlockSpec((bsz, tn), lambda j: (0, j)),
        out_shape=jax.ShapeDtypeStruct((bsz, n), _F32),
        compiler_params=pltpu.CompilerParams(
            dimension_semantics=("arbitrary",), vmem_limit_bytes=40 * 1024 * 1024),
    )(c, w, b.reshape(1, n))


def _mixer_kernel(x_ref, mod_ref, gmix_ref, win_ref, cw_ref, cb_ref, wa_ref, ba_ref, wi_ref, bi_ref,
                  lam_ref, gv_ref, wsp_ref, bspt_ref, o_ref, ext_ref, hprev_ref, *, ts, d_lru, heads, groups, chunk):
    s = pl.program_id(1)

    @pl.when(s == 0)
    def _():
        ext_ref[0:SUBLANES, :] = jnp.zeros((SUBLANES, d_lru), _F32)
        hprev_ref[...] = jnp.zeros_like(hprev_ref)

    x = x_ref[...]
    shift = mod_ref[0:1, :]
    scale = mod_ref[1:2, :]
    h = _rms(x, gmix_ref[...]) * (1.0 + scale) + shift
    hb = h.astype(_BF16)

    x_lru = _bdot(hb, win_ref[:, 0:d_lru])
    ext_ref[SUBLANES:SUBLANES + ts, :] = x_lru
    xc = cb_ref[...] + cw_ref[3:4, :] * x_lru
    for k in range(CONV_WIDTH - 1):
        off = SUBLANES - (CONV_WIDTH - 1) + k
        xc = xc + cw_ref[k:k + 1, :] * ext_ref[off:off + ts, :]
    ext_ref[0:SUBLANES, :] = x_lru[ts - SUBLANES:ts, :]

    xcb = xc.astype(_BF16)
    blk = d_lru // heads
    ra, ia = [], []
    for hh in range(heads):
        xh = xcb[:, hh * blk:(hh + 1) * blk]
        ra.append(_bdot(xh, wa_ref[hh]))
        ia.append(_bdot(xh, wi_ref[hh]))
    r = jax.nn.sigmoid(jnp.concatenate(ra, axis=1) + ba_ref[...])
    ig = jax.nn.sigmoid(jnp.concatenate(ia, axis=1) + bi_ref[...])
    log_a = (-LRU_C) * r * _softplus(-lam_ref[...])
    a = jnp.exp(log_a)
    bv = jnp.sqrt(-jnp.tanh(log_a) * (a * a + 1.0)) * (ig * xc)

    sub = lax.broadcasted_iota(_I32, (ts, d_lru), 0) % SUBLANES
    dist = 1
    while dist < SUBLANES:
        a_s = pltpu.roll(a, dist, 0)
        b_s = pltpu.roll(bv, dist, 0)
        m = sub >= dist
        bv = jnp.where(m, a * b_s + bv, bv)
        a = jnp.where(m, a * a_s, a)
        dist *= 2
    carry = hprev_ref[0:1, :]
    groups_h = []
    for gi in range(ts // SUBLANES):
        hg = bv[gi * SUBLANES:(gi + 1) * SUBLANES, :] + a[gi * SUBLANES:(gi + 1) * SUBLANES, :] * carry
        groups_h.append(hg)
        carry = hg[SUBLANES - 1:SUBLANES, :]
    hs = jnp.concatenate(groups_h, axis=0)
    hprev_ref[0:1, :] = carry

    y_gate = _bdot(hb, win_ref[:, d_lru:2 * d_lru])
    o_ref[:, 0:d_lru] = (hs * _gelu(y_gate)).astype(_BF16)

    d_sgu = gv_ref.shape[1]
    gd = d_sgu // groups
    u = _gelu(_bdot(hb, win_ref[:, 2 * d_lru:2 * d_lru + d_sgu]))
    v = _gelu(_bdot(hb, win_ref[:, 2 * d_lru + d_sgu:2 * d_lru + 2 * d_sgu]))
    vnb = _rms(v, gv_ref[...]).astype(_BF16)
    tri = lax.broadcasted_iota(_I32, (chunk, chunk), 0) >= lax.broadcasted_iota(_I32, (chunk, chunk), 1)
    for g in range(groups):
        wm = jnp.where(tri, wsp_ref[g], 0.0).astype(_BF16)
        bcol = bspt_ref[:, g:g + 1]
        for n in range(ts // chunk):
            sg = _bdot(wm, vnb[n * chunk:(n + 1) * chunk, g * gd:(g + 1) * gd]) + bcol
            ug = u[n * chunk:(n + 1) * chunk, g * gd:(g + 1) * gd]
            o_ref[n * chunk:(n + 1) * chunk, d_lru + g * gd:d_lru + (g + 1) * gd] = (ug * sg).astype(_BF16)


def _mixer_call(x, mod3, g_mix, w_in_b, conv_w, conv_b, w_a_b, b_a, w_i_b, b_i, lam, g_v, w_sp, b_sp_t, ts):
    bsz, seq, d = x.shape
    d_lru = conv_w.shape[1]
    d_sgu = g_v.shape[1]
    heads = w_a_b.shape[0]
    groups, chunk, _ = w_sp.shape
    d_mix = d_lru + d_sgu
    const = lambda shape: pl.BlockSpec(shape, lambda b, s: (0,) * len(shape), pipeline_mode=pl.Buffered(1))
    kern = functools.partial(_mixer_kernel, ts=ts, d_lru=d_lru, heads=heads, groups=groups, chunk=chunk)
    return pl.pallas_call(
        kern,
        grid=(bsz, seq // ts),
        in_specs=[
            pl.BlockSpec((None, ts, d), lambda b, s: (b, s, 0)),
            pl.BlockSpec((None, 6, d), lambda b, s: (b, 0, 0)),
            const((1, d)),
            const(w_in_b.shape),
            const(conv_w.shape),
            const((1, d_lru)),
            const(w_a_b.shape),
            const((1, d_lru)),
            const(w_i_b.shape),
            const((1, d_lru)),
            const((1, d_lru)),
            const((1, d_sgu)),
            const(w_sp.shape),
            const(b_sp_t.shape),
        ],
        out_specs=pl.BlockSpec((None, ts, d_mix), lambda b, s: (b, s, 0)),
        out_shape=jax.ShapeDtypeStruct((bsz, seq, d_mix), _BF16),
        scratch_shapes=[pltpu.VMEM((ts + SUBLANES, d_lru), _F32), pltpu.VMEM((SUBLANES, d_lru), _F32)],
        compiler_params=pltpu.CompilerParams(
            dimension_semantics=("arbitrary", "arbitrary"), vmem_limit_bytes=56 * 1024 * 1024),
    )(x, mod3, g_mix, w_in_b, conv_w, conv_b, w_a_b, b_a, w_i_b, b_i, lam, g_v, w_sp, b_sp_t)


def _topk_cols(vals, pos, ids, k, tick=None):
    n, t = vals.shape
    kio = lax.broadcasted_iota(_I32, (k, t), 0)
    out_v = jnp.zeros((k, t), _F32)
    out_i = jnp.zeros((k, t), _F32)
    for j in range(k):
        if tick is not None:
            tick()
        m = jnp.max(vals, axis=0, keepdims=True)
        p = jnp.min(jnp.where(vals == m, pos, _POS_INF), axis=0, keepdims=True)
        hit = pos == p
        e = p if ids is None else jnp.max(jnp.where(hit, ids, -1.0), axis=0, keepdims=True)
        out_v = jnp.where(kio == j, m, out_v)
        out_i = jnp.where(kio == j, e, out_i)
        vals = jnp.where(hit, _NEG_INF, vals)
    return out_v, out_i


def _pair_candidates(v1, i1, v2, i2, n_keys):
    k, t = v1.shape
    half = k // 2
    jrow = lax.broadcasted_iota(_I32, (half, t), 0)
    vals = [v1[0:1, :] + v2]
    pos = [lax.broadcasted_iota(_I32, (k, t), 0).astype(_F32)]
    ids = [i1[0:1, :] * n_keys + i2]
    for i in range(1, half):
        keep = jrow < k // (i + 1)
        vals.append(jnp.where(keep, v1[i:i + 1, :] + v2[0:half, :], _NEG_INF))
        pos.append((jrow + i * k).astype(_F32))
        ids.append(i1[i:i + 1, :] * n_keys + i2[0:half, :])
    vals.append(v1[half:k, :] + v2[0:1, :])
    pos.append(((jrow + half) * k).astype(_F32))
    ids.append(i1[half:k, :] * n_keys + i2[0:1, :])
    return jnp.concatenate(vals, axis=0), jnp.concatenate(pos, axis=0), jnp.concatenate(ids, axis=0)


def _route_kernel(y_ref, x_ref, mod_ref, wout_ref, gffn_ref, wq_ref, keys_ref,
                  x1_ref, h2_ref, sc_ref, *, heads):
    x1 = x_ref[...] + mod_ref[2:3, :] * _bdot(y_ref[...], wout_ref[...])
    x1_ref[...] = x1
    h2 = _rms(x1, gffn_ref[...]) * (1.0 + mod_ref[4:5, :]) + mod_ref[3:4, :]
    h2_ref[...] = h2
    q = _bdot(h2.astype(_BF16), wq_ref[...]).astype(_BF16)
    dk = keys_ref.shape[-1]
    nt = (((1,), (1,)), ((), ()))
    for c in range(2 * heads):
        sc_ref[c] = lax.dot_general(keys_ref[c // 2, c % 2], q[:, c * dk:(c + 1) * dk], nt,
                                    preferred_element_type=_F32)


def _route_call(y, x, mod3, w_out_b, g_ffn, w_q_b, keys_b, tq):
    bsz, seq, d = x.shape
    d_mix = y.shape[-1]
    heads, _, n_keys, dk = keys_b.shape
    nq = seq // tq
    const = lambda shape: pl.BlockSpec(shape, lambda b, s: (0,) * len(shape), pipeline_mode=pl.Buffered(1))
    kern = functools.partial(_route_kernel, heads=heads)
    return pl.pallas_call(
        kern,
        grid=(bsz, nq),
        in_specs=[
            pl.BlockSpec((None, tq, d_mix), lambda b, s: (b, s, 0)),
            pl.BlockSpec((None, tq, d), lambda b, s: (b, s, 0)),
            pl.BlockSpec((None, 6, d), lambda b, s: (b, 0, 0)),
            const(w_out_b.shape),
            const((1, d)),
            const(w_q_b.shape),
            const(keys_b.shape),
        ],
        out_specs=[
            pl.BlockSpec((None, tq, d), lambda b, s: (b, s, 0)),
            pl.BlockSpec((None, tq, d), lambda b, s: (b, s, 0)),
            pl.BlockSpec((2 * heads, n_keys, tq), lambda b, s: (0, 0, b * nq + s)),
        ],
        out_shape=[
            jax.ShapeDtypeStruct((bsz, seq, d), _F32),
            jax.ShapeDtypeStruct((bsz, seq, d), _F32),
            jax.ShapeDtypeStruct((2 * heads, n_keys, bsz * seq), _F32),
        ],
        compiler_params=pltpu.CompilerParams(
            dimension_semantics=("arbitrary", "arbitrary"), vmem_limit_bytes=56 * 1024 * 1024),
    )(y, x, mod3, w_out_b, g_ffn, w_q_b, keys_b)


def _pack_kernel(u_ref, v_ref, o_ref):
    w = pltpu.pack_elementwise([u_ref[...], v_ref[...]], packed_dtype=_BF16)
    for c in range(o_ref.shape[1]):
        o_ref[:, c, :] = w[:, c * LANES:(c + 1) * LANES]


def _pack_call(expert_u, expert_v, rows):
    n, d = expert_u.shape
    spec = pl.BlockSpec((rows, d), lambda i: (i, 0))
    return pl.pallas_call(
        _pack_kernel,
        grid=(n // rows,),
        in_specs=[spec, spec],
        out_specs=pl.BlockSpec((rows, d // LANES, LANES), lambda i: (i, 0, 0)),
        out_shape=jax.ShapeDtypeStruct((n, d // LANES, LANES), jnp.uint32),
        compiler_params=pltpu.CompilerParams(
            dimension_semantics=("arbitrary",), vmem_limit_bytes=40 * 1024 * 1024),
    )(expert_u, expert_v)


def _peer_kernel(sc_ref, h_ref, x1_ref, mod_ref, gfin_ref, tab_ref, tabw_ref, o_ref,
                 buf0_ref, buf1_ref, sem_ref, zero_ref, idxv_ref, gt_ref, gs_ref, idxs_ref, ssem_ref,
                 *, tb, nsel, d, nsb, n_keys):
    s = pl.program_id(0)
    k = pl.program_id(1)
    steps = pl.num_programs(1)
    nct = d // LANES
    tok_tiles = (nsel // SUBLANES) * nct
    bufs = (buf0_ref, buf1_ref)
    sb_tokens = sc_ref.shape[-1]
    nsteps = sb_tokens // tb

    def row_copy(row, sl, t, e):
        first = t * tok_tiles + (e // SUBLANES) * nct
        return pltpu.make_async_copy(
            tab_ref.at[row], bufs[sl].at[pl.ds(first, nct), e % SUBLANES, :], sem_ref.at[sl, t])

    def token_wait(sl, t):
        pltpu.make_async_copy(
            tabw_ref.at[pl.ds(0, tok_tiles)], bufs[sl].at[pl.ds(t * tok_tiles, tok_tiles)], sem_ref.at[sl, t]).wait()

    key_pos = lax.broadcasted_iota(_I32, (n_keys, sb_tokens), 0).astype(_F32)

    def route_head(tick):
        tops = [_topk_cols(sc_ref[p], key_pos, None, PEER_TOPK, tick) for p in range(2)]
        (v1, i1), (v2, i2) = tops
        cand, cpos, cids = _pair_candidates(v1, i1, v2, i2, float(n_keys))
        tv, ti = _topk_cols(cand, cpos, cids, PEER_TOPK, tick)
        ex = jnp.exp(tv - tv[0:1, :])
        r = pl.multiple_of(k * PEER_TOPK, PEER_TOPK)
        idxv_ref[pl.ds(r, PEER_TOPK), :] = ti.astype(_I32)
        gt_ref[pl.ds(r, PEER_TOPK), :] = ex / jnp.sum(ex, axis=0, keepdims=True)

    def ids_to_smem(slot):
        return pltpu.make_async_copy(idxv_ref, idxs_ref.at[slot], ssem_ref)

    def publish():
        @pl.when(k == steps - 1)
        def _():
            gs_ref[s % 3] = gt_ref[...].T
            ids_to_smem(s % 3).start()

    def published():
        @pl.when(jnp.logical_and(k == 0, s > 0))
        def _():
            ids_to_smem((s - 1) % 3).wait()

    ngrp = nsel // SUBLANES
    rows8 = lax.broadcasted_iota(_I32, (SUBLANES, 2 * nsel), 0)
    lane = lax.broadcasted_iota(_I32, (nsel, LANES), 1)
    spread = (lax.broadcasted_iota(_I32, (nsel, 2 * nsel), 1)
              == 2 * lax.broadcasted_iota(_I32, (nsel, 2 * nsel), 0) + 1).astype(_BF16)
    final = jnp.logical_and(s == nsb + 1, k == steps - 1)
    nxt = jnp.where(final, k, k + 1)
    nslot = (s - 2 + nxt // nsteps) % 3
    ntok0 = (nxt % nsteps) * tb

    def next_row(t, e):
        return idxs_ref[nslot, e, ntok0 + t]

    def token_matrix(sl, t):
        w = bufs[sl][t * tok_tiles:(t + 1) * tok_tiles]
        rows = [jnp.concatenate([w[g * nct + c] for c in range(nct)], axis=1) for g in range(ngrp)]
        return pltpu.bitcast(jnp.concatenate(rows, axis=0), _BF16)

    def u_dots(sl, t):
        acc = [jnp.zeros((SUBLANES, LANES), _F32) for _ in range(ngrp)]
        for c in range(nct):
            xc = h_ref[t:t + 1, c * LANES:(c + 1) * LANES] + zero_ref[...]
            for g in range(ngrp):
                w = bufs[sl][t * tok_tiles + g * nct + c]
                u = pltpu.unpack_elementwise(w, index=0, packed_dtype=_BF16, unpacked_dtype=_F32)
                acc[g] = acc[g] + u * xc
        return jnp.sum(jnp.concatenate(acc, axis=0), axis=1, keepdims=True)

    def evaluate(sl):
        nsl = 1 - sl
        ngroups = tb // SUBLANES
        copies = [(t, e) for t in range(tb) for e in range(nsel)]
        nticks = 3 * PEER_TOPK
        per_tick = len(copies) // (6 * nticks)
        issued = [0]

        def issue(n):
            for t, e in copies[issued[0]:issued[0] + n]:
                row_copy(next_row(t, e), nsl, t, e).start(priority=e % 2)
            issued[0] += n

        published()
        route_head(lambda: issue(per_tick))
        publish()
        nregions = (ngroups + 1) * SUBLANES
        share = -(-(len(copies) - issued[0]) // nregions)
        gslot = (s - 2) % 3
        act2 = None
        for p in range(ngroups + 1):
            r0, v0 = p * SUBLANES, (p - 1) * SUBLANES
            zcols = jnp.zeros((nsel, LANES), _F32)
            out = jnp.zeros((SUBLANES, d), _F32)
            for j in range(SUBLANES):
                if p < ngroups:
                    token_wait(sl, r0 + j)
                issue(share)
                if p > 0:
                    aj = jnp.where(rows8 == j, act2, 0.0).astype(_BF16)
                    out = out + _bdot(aj, token_matrix(sl, v0 + j))
                if p < ngroups:
                    zcols = jnp.where(lane == j, u_dots(sl, r0 + j), zcols)
            if p > 0:
                x2 = x1_ref[v0:v0 + SUBLANES, :] + mod_ref[5:6, :] * out
                o_ref[v0:v0 + SUBLANES, :] = _rms(x2, gfin_ref[...])
            if p < ngroups:
                z = zcols.T[0:SUBLANES, :]
                g_rows = gs_ref[gslot, pl.ds(pl.multiple_of(k * tb + r0, SUBLANES), SUBLANES), :]
                act = (_gelu(z) * g_rows).astype(_BF16)
                act2 = _bdot(act, spread)

    @pl.when(s < 2)
    def _():
        @pl.when(jnp.logical_and(s == 0, k == 0))
        def _():
            zero_ref[...] = jnp.zeros_like(zero_ref)

        published()
        route_head(None)
        publish()

        @pl.when(jnp.logical_and(s == 1, k == steps - 1))
        def _():
            def body(t, c):
                for e in range(nsel):
                    row_copy(idxs_ref[0, e, t], 0, t, e).start(priority=e % 2)
                return c
            lax.fori_loop(0, tb, body, 0)

    for sl in range(2):
        @pl.when(jnp.logical_and(s >= 2, k % 2 == sl))
        def _(sl=sl):
            evaluate(sl)

    @pl.when(final)
    def _():
        ids_to_smem(s % 3).wait()
        for t in range(tb):
            token_wait(nsteps % 2, t)


def _peer_call(sc, h2, x1, mod3, g_final, tab3, tb, seq, nsel):
    nhh, n_keys, t_total = sc.shape
    d = h2.shape[-1]
    sb_tokens = LANES
    nsteps = sb_tokens // tb
    assert nhh == 2 * nsteps, "one routing head per evaluation step"
    nsb = t_total // sb_tokens
    nct = d // LANES
    n_exp = tab3.shape[0]
    tabw = tab3.reshape(n_exp * nct // SUBLANES, SUBLANES, LANES)
    slot_tiles = tb * (nsel // SUBLANES) * nct
    blk = lambda s, k: jnp.maximum((s - 2) * nsteps + k, 0)
    kern = functools.partial(_peer_kernel, tb=tb, nsel=nsel, d=d, nsb=nsb, n_keys=n_keys)
    return pl.pallas_call(
        kern,
        grid=(nsb + 2, nsteps),
        in_specs=[
            pl.BlockSpec((2, n_keys, sb_tokens), lambda s, k: (k, 0, jnp.minimum(s, nsb - 1))),
            pl.BlockSpec((tb, d), lambda s, k: (blk(s, k), 0)),
            pl.BlockSpec((tb, d), lambda s, k: (blk(s, k), 0)),
            pl.BlockSpec((None, 6, d), lambda s, k: ((blk(s, k) * tb) // seq, 0, 0)),
            pl.BlockSpec((1, d), lambda s, k: (0, 0)),
            pl.BlockSpec(memory_space=pl.ANY),
            pl.BlockSpec(memory_space=pl.ANY),
        ],
        out_specs=pl.BlockSpec((tb, d), lambda s, k: (blk(s, k), 0)),
        out_shape=jax.ShapeDtypeStruct((t_total, d), _F32),
        scratch_shapes=[
            pltpu.VMEM((slot_tiles, SUBLANES, LANES), jnp.uint32),
            pltpu.VMEM((slot_tiles, SUBLANES, LANES), jnp.uint32),
            pltpu.SemaphoreType.DMA((2, tb)),
            pltpu.VMEM((SUBLANES, LANES), _F32),
            pltpu.VMEM((nsel, sb_tokens), _I32),
            pltpu.VMEM((nsel, sb_tokens), _F32),
            pltpu.VMEM((3, sb_tokens, nsel), _F32),
            pltpu.SMEM((3, nsel, sb_tokens), _I32),
            pltpu.SemaphoreType.DMA(()),
        ],
        compiler_params=pltpu.CompilerParams(
            dimension_semantics=("arbitrary", "arbitrary"), vmem_limit_bytes=56 * 1024 * 1024),
    )(sc, h2, x1, mod3, g_final, tab3, tabw)


def _layer(x, mod3, g_norm_mix, w_in, conv_w, conv_b, w_gate_a, b_gate_a, w_gate_i, b_gate_i, lru_lambda, g_v,
           w_spatial, b_spatial, w_out, g_norm_ffn, w_query, sub_keys, expert_u, expert_v, g_out, ts, tq, tb):
    bsz, seq, d = x.shape
    d_lru = conv_w.shape[-1]
    d_sgu = g_v.shape[-1]
    y = _mixer_call(
        x, mod3, g_norm_mix.reshape(1, d), w_in.astype(_BF16), conv_w, conv_b.reshape(1, d_lru),
        w_gate_a.astype(_BF16), b_gate_a.reshape(1, d_lru), w_gate_i.astype(_BF16), b_gate_i.reshape(1, d_lru),
        lru_lambda.reshape(1, d_lru), g_v.reshape(1, d_sgu), w_spatial, b_spatial.T, ts)
    x1, h2, sc = _route_call(
        y, x, mod3, w_out.astype(_BF16), g_norm_ffn.reshape(1, d), w_query.astype(_BF16), sub_keys.astype(_BF16), tq)
    t_total = bsz * seq
    nsel = sub_keys.shape[0] * PEER_TOPK
    tab = _pack_call(expert_u, expert_v, min(256, expert_u.shape[0]))
    out = _peer_call(sc, h2.reshape(t_total, d), x1.reshape(t_total, d), mod3, g_out.reshape(1, d), tab, tb, seq, nsel)
    return out.reshape(bsz, seq, d)


def kernel(x, c, w_ada, b_ada, g_norm_mix, w_in, conv_w, conv_b, w_gate_a, b_gate_a, w_gate_i, b_gate_i,
           lru_lambda, g_v, w_spatial, b_spatial, w_out, g_norm_ffn, w_query, sub_keys, expert_u, expert_v, g_final):
    depth = w_ada.shape[0]
    assert depth == 1, "the final RMSNorm is fused into the layer's last call"
    bsz, seq, d = x.shape
    ts = min(256, seq)
    tq = min(256, seq)
    tb = 16
    n_mod = w_ada.shape[-1]
    mod = _ada_call(c, w_ada[0], b_ada[0], 1024 if n_mod % 1024 == 0 else n_mod)
    mod3 = mod.reshape(bsz, 6, d)
    return _layer(x, mod3, g_norm_mix[0], w_in[0], conv_w[0], conv_b[0], w_gate_a[0], b_gate_a[0], w_gate_i[0],
                  b_gate_i[0], lru_lambda[0], g_v[0], w_spatial[0], b_spatial[0], w_out[0], g_norm_ffn[0],
                  w_query[0], sub_keys[0], expert_u[0], expert_v[0], g_final, ts, tq, tb)
```

```python
import functools

import jax
import jax.numpy as jnp
from jax import lax
from jax.experimental import pallas as pl
from jax.experimental.pallas import tpu as pltpu

_F32 = jnp.float32
_BF16 = jnp.bfloat16
_I32 = jnp.int32

EPS = 1e-6
LRU_C = 8.0
CONV_WIDTH = 4
PEER_TOPK = 16
LANES = 128
SUBLANES = 8
_NEG_INF = float("-inf")
_POS_INF = float("inf")


def _gelu(x):
    return 0.5 * x * (1.0 + jnp.tanh(0.7978845608028654 * (x + 0.044715 * (x * x * x))))


def _softplus(y):
    return jnp.maximum(y, 0.0) + jnp.log1p(jnp.exp(-jnp.abs(y)))


def _rms(x, g):
    return x * lax.rsqrt(jnp.mean(x * x, axis=-1, keepdims=True) + EPS) * g


def _bdot(a, b):
    return jnp.dot(a, b, preferred_element_type=_F32)


def _ada_kernel(c_ref, w_ref, b_ref, o_ref):
    c = c_ref[...]
    a = c * jax.nn.sigmoid(c)
    o_ref[...] = _bdot(a.astype(_BF16), w_ref[...].astype(_BF16)) + b_ref[...]


def _ada_call(c, w, b, tn):
    bsz, d = c.shape
    n = w.shape[1]
    return pl.pallas_call(
        _ada_kernel,
        grid=(n // tn,),
        in_specs=[
            pl.BlockSpec((bsz, d), lambda j: (0, 0)),
            pl.BlockSpec((d, tn), lambda j: (0, j)),
            pl.BlockSpec((1, tn), lambda j: (0, j)),
        ],
        out_specs=pl.BlockSpec((bsz, tn), lambda j: (0, j)),
        out_shape=jax.ShapeDtypeStruct((bsz, n), _F32),
        compiler_params=pltpu.CompilerParams(
            dimension_semantics=("arbitrary",), vmem_limit_bytes=40 * 1024 * 1024),
    )(c, w, b.reshape(1, n))


def _mixer_kernel(x_ref, mod_ref, gmix_ref, win_ref, cw_ref, cb_ref, wa_ref, ba_ref, wi_ref, bi_ref,
                  lam_ref, gv_ref, wsp_ref, bspt_ref, o_ref, ext_ref, hprev_ref, *, ts, d_lru, heads, groups, chunk):
    s = pl.program_id(1)

    @pl.when(s == 0)
    def _():
        ext_ref[0:SUBLANES, :] = jnp.zeros((SUBLANES, d_lru), _F32)
        hprev_ref[...] = jnp.zeros_like(hprev_ref)

    x = x_ref[...]
    shift = mod_ref[0:1, :]
    scale = mod_ref[1:2, :]
    h = _rms(x, gmix_ref[...]) * (1.0 + scale) + shift
    hb = h.astype(_BF16)

    x_lru = _bdot(hb, win_ref[:, 0:d_lru])
    ext_ref[SUBLANES:SUBLANES + ts, :] = x_lru
    xc = cb_ref[...] + cw_ref[3:4, :] * x_lru
    for k in range(CONV_WIDTH - 1):
        off = SUBLANES - (CONV_WIDTH - 1) + k
        xc = xc + cw_ref[k:k + 1, :] * ext_ref[off:off + ts, :]
    ext_ref[0:SUBLANES, :] = x_lru[ts - SUBLANES:ts, :]

    xcb = xc.astype(_BF16)
    blk = d_lru // heads
    ra, ia = [], []
    for hh in range(heads):
        xh = xcb[:, hh * blk:(hh + 1) * blk]
        ra.append(_bdot(xh, wa_ref[hh]))
        ia.append(_bdot(xh, wi_ref[hh]))
    r = jax.nn.sigmoid(jnp.concatenate(ra, axis=1) + ba_ref[...])
    ig = jax.nn.sigmoid(jnp.concatenate(ia, axis=1) + bi_ref[...])
    log_a = (-LRU_C) * r * _softplus(-lam_ref[...])
    a = jnp.exp(log_a)
    bv = jnp.sqrt(-jnp.tanh(log_a) * (a * a + 1.0)) * (ig * xc)

    sub = lax.broadcasted_iota(_I32, (ts, d_lru), 0) % SUBLANES
    dist = 1
    while dist < SUBLANES:
        a_s = pltpu.roll(a, dist, 0)
        b_s = pltpu.roll(bv, dist, 0)
        m = sub >= dist
        bv = jnp.where(m, a * b_s + bv, bv)
        a = jnp.where(m, a * a_s, a)
        dist *= 2
    carry = hprev_ref[0:1, :]
    groups_h = []
    for gi in range(ts // SUBLANES):
        hg = bv[gi * SUBLANES:(gi + 1) * SUBLANES, :] + a[gi * SUBLANES:(gi + 1) * SUBLANES, :] * carry
        groups_h.append(hg)
        carry = hg[SUBLANES - 1:SUBLANES, :]
    hs = jnp.concatenate(groups_h, axis=0)
    hprev_ref[0:1, :] = carry

    y_gate = _bdot(hb, win_ref[:, d_lru:2 * d_lru])
    o_ref[:, 0:d_lru] = (hs * _gelu(y_gate)).astype(_BF16)

    d_sgu = gv_ref.shape[1]
    gd = d_sgu // groups
    u = _gelu(_bdot(hb, win_ref[:, 2 * d_lru:2 * d_lru + d_sgu]))
    v = _gelu(_bdot(hb, win_ref[:, 2 * d_lru + d_sgu:2 * d_lru + 2 * d_sgu]))
    vnb = _rms(v, gv_ref[...]).astype(_BF16)
    tri = lax.broadcasted_iota(_I32, (chunk, chunk), 0) >= lax.broadcasted_iota(_I32, (chunk, chunk), 1)
    for g in range(groups):
        wm = jnp.where(tri, wsp_ref[g], 0.0).astype(_BF16)
        bcol = bspt_ref[:, g:g + 1]
        for n in range(ts // chunk):
            sg = _bdot(wm, vnb[n * chunk:(n + 1) * chunk, g * gd:(g + 1) * gd]) + bcol
            ug = u[n * chunk:(n + 1) * chunk, g * gd:(g + 1) * gd]
            o_ref[n * chunk:(n + 1) * chunk, d_lru + g * gd:d_lru + (g + 1) * gd] = (ug * sg).astype(_BF16)


def _mixer_call(x, mod3, g_mix, w_in_b, conv_w, conv_b, w_a_b, b_a, w_i_b, b_i, lam, g_v, w_sp, b_sp_t, ts):
    bsz, seq, d = x.shape
    d_lru = conv_w.shape[1]
    d_sgu = g_v.shape[1]
    heads = w_a_b.shape[0]
    groups, chunk, _ = w_sp.shape
    d_mix = d_lru + d_sgu
    const = lambda shape: pl.BlockSpec(shape, lambda b, s: (0,) * len(shape), pipeline_mode=pl.Buffered(1))
    kern = functools.partial(_mixer_kernel, ts=ts, d_lru=d_lru, heads=heads, groups=groups, chunk=chunk)
    return pl.pallas_call(
        kern,
        grid=(bsz, seq // ts),
        in_specs=[
            pl.BlockSpec((None, ts, d), lambda b, s: (b, s, 0)),
            pl.BlockSpec((None, 6, d), lambda b, s: (b, 0, 0)),
            const((1, d)),
            const(w_in_b.shape),
            const(conv_w.shape),
            const((1, d_lru)),
            const(w_a_b.shape),
            const((1, d_lru)),
            const(w_i_b.shape),
            const((1, d_lru)),
            const((1, d_lru)),
            const((1, d_sgu)),
            const(w_sp.shape),
            const(b_sp_t.shape),
        ],
        out_specs=pl.BlockSpec((None, ts, d_mix), lambda b, s: (b, s, 0)),
        out_shape=jax.ShapeDtypeStruct((bsz, seq, d_mix), _BF16),
        scratch_shapes=[pltpu.VMEM((ts + SUBLANES, d_lru), _F32), pltpu.VMEM((SUBLANES, d_lru), _F32)],
        compiler_params=pltpu.CompilerParams(
            dimension_semantics=("arbitrary", "arbitrary"), vmem_limit_bytes=56 * 1024 * 1024),
    )(x, mod3, g_mix, w_in_b, conv_w, conv_b, w_a_b, b_a, w_i_b, b_i, lam, g_v, w_sp, b_sp_t)


def _topk_cols(vals, pos, ids, k, tick=None):
    n, t = vals.shape
    kio = lax.broadcasted_iota(_I32, (k, t), 0)
    out_v = jnp.zeros((k, t), _F32)
    out_i = jnp.zeros((k, t), _F32)
    for j in range(k):
        if tick is not None:
            tick()
        m = jnp.max(vals, axis=0, keepdims=True)
        p = jnp.min(jnp.where(vals == m, pos, _POS_INF), axis=0, keepdims=True)
        hit = pos == p
        e = p if ids is None else jnp.max(jnp.where(hit, ids, -1.0), axis=0, keepdims=True)
        out_v = jnp.where(kio == j, m, out_v)
        out_i = jnp.where(kio == j, e, out_i)
        vals = jnp.where(hit, _NEG_INF, vals)
    return out_v, out_i


def _pair_candidates(v1, i1, v2, i2, n_keys):
    k, t = v1.shape
    half = k // 2
    jrow = lax.broadcasted_iota(_I32, (half, t), 0)
    vals = [v1[0:1, :] + v2]
    pos = [lax.broadcasted_iota(_I32, (k, t), 0).astype(_F32)]
    ids = [i1[0:1, :] * n_keys + i2]
    for i in range(1, half):
        keep = jrow < k // (i + 1)
        vals.append(jnp.where(keep, v1[i:i + 1, :] + v2[0:half, :], _NEG_INF))
        pos.append((jrow + i * k).astype(_F32))
        ids.append(i1[i:i + 1, :] * n_keys + i2[0:half, :])
    vals.append(v1[half:k, :] + v2[0:1, :])
    pos.append(((jrow + half) * k).astype(_F32))
    ids.append(i1[half:k, :] * n_keys + i2[0:1, :])
    return jnp.concatenate(vals, axis=0), jnp.concatenate(pos, axis=0), jnp.concatenate(ids, axis=0)


def _route_kernel(y_ref, x_ref, mod_ref, wout_ref, gffn_ref, wq_ref, keys_ref,
                  x1_ref, h2_ref, sc_ref, *, heads):
    x1 = x_ref[...] + mod_ref[2:3, :] * _bdot(y_ref[...], wout_ref[...])
    x1_ref[...] = x1
    h2 = _rms(x1, gffn_ref[...]) * (1.0 + mod_ref[4:5, :]) + mod_ref[3:4, :]
    h2_ref[...] = h2
    q = _bdot(h2.astype(_BF16), wq_ref[...]).astype(_BF16)
    dk = keys_ref.shape[-1]
    nt = (((1,), (1,)), ((), ()))
    for c in range(2 * heads):
        sc_ref[c] = lax.dot_general(keys_ref[c // 2, c % 2], q[:, c * dk:(c + 1) * dk], nt,
                                    preferred_element_type=_F32)


def _route_call(y, x, mod3, w_out_b, g_ffn, w_q_b, keys_b, tq):
    bsz, seq, d = x.shape
    d_mix = y.shape[-1]
    heads, _, n_keys, dk = keys_b.shape
    nq = seq // tq
    const = lambda shape: pl.BlockSpec(shape, lambda b, s: (0,) * len(shape), pipeline_mode=pl.Buffered(1))
    kern = functools.partial(_route_kernel, heads=heads)
    return pl.pallas_call(
        kern,
        grid=(bsz, nq),
        in_specs=[
            pl.BlockSpec((None, tq, d_mix), lambda b, s: (b, s, 0)),
            pl.BlockSpec((None, tq, d), lambda b, s: (b, s, 0)),
            pl.BlockSpec((None, 6, d), lambda b, s: (b, 0, 0)),
            const(w_out_b.shape),
            const((1, d)),
            const(w_q_b.shape),
            const(keys_b.shape),
        ],
        out_specs=[
            pl.BlockSpec((None, tq, d), lambda b, s: (b, s, 0)),
            pl.BlockSpec((None, tq, d), lambda b, s: (b, s, 0)),
            pl.BlockSpec((2 * heads, n_keys, tq), lambda b, s: (0, 0, b * nq + s)),
        ],
        out_shape=[
            jax.ShapeDtypeStruct((bsz, seq, d), _F32),
            jax.ShapeDtypeStruct((bsz, seq, d), _F32),
            jax.ShapeDtypeStruct((2 * heads, n_keys, bsz * seq), _F32),
        ],
        compiler_params=pltpu.CompilerParams(
            dimension_semantics=("arbitrary", "arbitrary"), vmem_limit_bytes=56 * 1024 * 1024),
    )(y, x, mod3, w_out_b, g_ffn, w_q_b, keys_b)


def _pack_kernel(u_ref, v_ref, o_ref):
    w = pltpu.pack_elementwise([u_ref[...], v_ref[...]], packed_dtype=_BF16)
    for c in range(o_ref.shape[1]):
        o_ref[:, c, :] = w[:, c * LANES:(c + 1) * LANES]


def _pack_call(expert_u, expert_v, rows):
    n, d = expert_u.shape
    spec = pl.BlockSpec((rows, d), lambda i: (i, 0))
    return pl.pallas_call(
        _pack_kernel,
        grid=(n // rows,),
        in_specs=[spec, spec],
        out_specs=pl.BlockSpec((rows, d // LANES, LANES), lambda i: (i, 0, 0)),
        out_shape=jax.ShapeDtypeStruct((n, d // LANES, LANES), jnp.uint32),
        compiler_params=pltpu.CompilerParams(
            dimension_semantics=("arbitrary",), vmem_limit_bytes=40 * 1024 * 1024),
    )(expert_u, expert_v)


def _peer_kernel(sc_ref, h_ref, x1_ref, mod_ref, gfin_ref, tab_ref, tabw_ref, o_ref,
                 buf0_ref, buf1_ref, sem_ref, zero_ref, idxv_ref, gt_ref, gs_ref, idxs_ref, ssem_ref,
                 *, tb, nsel, d, nsb, n_keys):
    s = pl.program_id(0)
    k = pl.program_id(1)
    steps = pl.num_programs(1)
    nct = d // LANES
    tok_tiles = (nsel // SUBLANES) * nct
    bufs = (buf0_ref, buf1_ref)
    sb_tokens = sc_ref.shape[-1]
    nblocks = sb_tokens // tb

    def row_copy(row, sl, t, e):
        first = t * tok_tiles + (e // SUBLANES) * nct
        return pltpu.make_async_copy(
            tab_ref.at[row], bufs[sl].at[pl.ds(first, nct), e % SUBLANES, :], sem_ref.at[sl, t])

    def token_wait(sl, t):
        pltpu.make_async_copy(
            tabw_ref.at[pl.ds(0, tok_tiles)], bufs[sl].at[pl.ds(t * tok_tiles, tok_tiles)], sem_ref.at[sl, t]).wait()

    key_pos = lax.broadcasted_iota(_I32, (n_keys, sb_tokens), 0).astype(_F32)

    def route_head(hl, tick):
        tops = [_topk_cols(sc_ref[2 * hl + p], key_pos, None, PEER_TOPK, tick) for p in range(2)]
        (v1, i1), (v2, i2) = tops
        cand, cpos, cids = _pair_candidates(v1, i1, v2, i2, float(n_keys))
        tv, ti = _topk_cols(cand, cpos, cids, PEER_TOPK, tick)
        ex = jnp.exp(tv - tv[0:1, :])
        r = pl.multiple_of((2 * k + hl) * PEER_TOPK, PEER_TOPK)
        idxv_ref[pl.ds(r, PEER_TOPK), :] = ti.astype(_I32)
        gt_ref[pl.ds(r, PEER_TOPK), :] = ex / jnp.sum(ex, axis=0, keepdims=True)

    def ids_to_smem(slot):
        return pltpu.make_async_copy(idxv_ref, idxs_ref.at[slot], ssem_ref)

    def publish():
        @pl.when(k == steps - 1)
        def _():
            gs_ref[s % 3] = gt_ref[...].T
            ids_to_smem(s % 3).start()

    def published():
        @pl.when(jnp.logical_and(k == 0, s > 0))
        def _():
            ids_to_smem((s - 1) % 3).wait()

    ngrp = nsel // SUBLANES
    rows8 = lax.broadcasted_iota(_I32, (SUBLANES, 2 * nsel), 0)
    lane = lax.broadcasted_iota(_I32, (nsel, LANES), 1)
    spread = (lax.broadcasted_iota(_I32, (nsel, 2 * nsel), 1)
              == 2 * lax.broadcasted_iota(_I32, (nsel, 2 * nsel), 0) + 1).astype(_BF16)
    final = jnp.logical_and(s == nsb + 1, k == steps - 1)

    def token_matrix(sl, t):
        w = bufs[sl][t * tok_tiles:(t + 1) * tok_tiles]
        rows = [jnp.concatenate([w[g * nct + c] for c in range(nct)], axis=1) for g in range(ngrp)]
        return pltpu.bitcast(jnp.concatenate(rows, axis=0), _BF16)

    def u_dots(sl, t):
        acc = [jnp.zeros((SUBLANES, LANES), _F32) for _ in range(ngrp)]
        for c in range(nct):
            xc = h_ref[sl * tb + t:sl * tb + t + 1, c * LANES:(c + 1) * LANES] + zero_ref[...]
            for g in range(ngrp):
                w = bufs[sl][t * tok_tiles + g * nct + c]
                u = pltpu.unpack_elementwise(w, index=0, packed_dtype=_BF16, unpacked_dtype=_F32)
                acc[g] = acc[g] + u * xc
        return jnp.sum(jnp.concatenate(acc, axis=0), axis=1, keepdims=True)

    def evaluate(sl):
        nsl = 1 - sl
        blk = 2 * k + sl
        nxt = jnp.where(final, blk, blk + 1) if sl == 1 else blk + 1
        nslot = (s - 2 + nxt // nblocks) % 3
        ntok0 = (nxt % nblocks) * tb
        row0 = sl * tb

        def next_row(t, e):
            return idxs_ref[nslot, e, ntok0 + t]
        ngroups = tb // SUBLANES
        copies = [(t, e) for t in range(tb) for e in range(nsel)]
        nticks = 3 * PEER_TOPK
        per_tick = len(copies) // (6 * nticks)
        issued = [0]

        def issue(n):
            for t, e in copies[issued[0]:issued[0] + n]:
                row_copy(next_row(t, e), nsl, t, e).start(priority=e % 2)
            issued[0] += n

        token_wait(sl, 0)
        if sl == 0:
            published()
        route_head(sl, lambda: issue(per_tick))
        if sl == 1:
            publish()
        nregions = (ngroups + 1) * SUBLANES
        share = -(-(len(copies) - issued[0]) // nregions)
        gslot = (s - 2) % 3
        act2 = None
        for p in range(ngroups + 1):
            r0, v0 = p * SUBLANES, (p - 1) * SUBLANES
            zcols = jnp.zeros((nsel, LANES), _F32)
            out = jnp.zeros((SUBLANES, d), _F32)
            for j in range(SUBLANES):
                if p < ngroups and r0 + j > 0:
                    token_wait(sl, r0 + j)
                issue(share)
                if p > 0:
                    aj = jnp.where(rows8 == j, act2, 0.0).astype(_BF16)
                    out = out + _bdot(aj, token_matrix(sl, v0 + j))
                if p < ngroups:
                    zcols = jnp.where(lane == j, u_dots(sl, r0 + j), zcols)
            if p > 0:
                x2 = x1_ref[row0 + v0:row0 + v0 + SUBLANES, :] + mod_ref[5:6, :] * out
                o_ref[row0 + v0:row0 + v0 + SUBLANES, :] = _rms(x2, gfin_ref[...])
            if p < ngroups:
                z = zcols.T[0:SUBLANES, :]
                g_rows = gs_ref[gslot, pl.ds(pl.multiple_of(blk * tb + r0, SUBLANES), SUBLANES), :]
                act = (_gelu(z) * g_rows).astype(_BF16)
                act2 = _bdot(act, spread)

    @pl.when(s < 2)
    def _():
        @pl.when(jnp.logical_and(s == 0, k == 0))
        def _():
            zero_ref[...] = jnp.zeros_like(zero_ref)

        published()
        route_head(0, None)
        route_head(1, None)
        publish()

        @pl.when(jnp.logical_and(s == 1, k == steps - 1))
        def _():
            def body(t, c):
                for e in range(nsel):
                    row_copy(idxs_ref[0, e, t], 0, t, e).start(priority=e % 2)
                return c
            lax.fori_loop(0, tb, body, 0)

    @pl.when(s >= 2)
    def _():
        evaluate(0)
        evaluate(1)

    @pl.when(final)
    def _():
        ids_to_smem(s % 3).wait()
        for t in range(tb):
            token_wait(0, t)


def _peer_call(sc, h2, x1, mod3, g_final, tab3, tb, seq, nsel):
    nhh, n_keys, t_total = sc.shape
    d = h2.shape[-1]
    sb_tokens = LANES
    nsteps = sb_tokens // (2 * tb)
    assert nhh == 4 * nsteps, "two routing heads per evaluation step"
    nsb = t_total // sb_tokens
    nct = d // LANES
    n_exp = tab3.shape[0]
    tabw = tab3.reshape(n_exp * nct // SUBLANES, SUBLANES, LANES)
    slot_tiles = tb * (nsel // SUBLANES) * nct
    blk = lambda s, k: jnp.maximum((s - 2) * nsteps + k, 0)
    kern = functools.partial(_peer_kernel, tb=tb, nsel=nsel, d=d, nsb=nsb, n_keys=n_keys)
    return pl.pallas_call(
        kern,
        grid=(nsb + 2, nsteps),
        in_specs=[
            pl.BlockSpec((4, n_keys, sb_tokens), lambda s, k: (k, 0, jnp.minimum(s, nsb - 1))),
            pl.BlockSpec((2 * tb, d), lambda s, k: (blk(s, k), 0)),
            pl.BlockSpec((2 * tb, d), lambda s, k: (blk(s, k), 0)),
            pl.BlockSpec((None, 6, d), lambda s, k: ((blk(s, k) * 2 * tb) // seq, 0, 0)),
            pl.BlockSpec((1, d), lambda s, k: (0, 0)),
            pl.BlockSpec(memory_space=pl.ANY),
            pl.BlockSpec(memory_space=pl.ANY),
        ],
        out_specs=pl.BlockSpec((2 * tb, d), lambda s, k: (blk(s, k), 0)),
        out_shape=jax.ShapeDtypeStruct((t_total, d), _F32),
        scratch_shapes=[
            pltpu.VMEM((slot_tiles, SUBLANES, LANES), jnp.uint32),
            pltpu.VMEM((slot_tiles, SUBLANES, LANES), jnp.uint32),
            pltpu.SemaphoreType.DMA((2, tb)),
            pltpu.VMEM((SUBLANES, LANES), _F32),
            pltpu.VMEM((nsel, sb_tokens), _I32),
            pltpu.VMEM((nsel, sb_tokens), _F32),
            pltpu.VMEM((3, sb_tokens, nsel), _F32),
            pltpu.SMEM((3, nsel, sb_tokens), _I32),
            pltpu.SemaphoreType.DMA(()),
        ],
        compiler_params=pltpu.CompilerParams(
            dimension_semantics=("arbitrary", "arbitrary"), vmem_limit_bytes=56 * 1024 * 1024),
    )(sc, h2, x1, mod3, g_final, tab3, tabw)


def _layer(x, mod3, g_norm_mix, w_in, conv_w, conv_b, w_gate_a, b_gate_a, w_gate_i, b_gate_i, lru_lambda, g_v,
           w_spatial, b_spatial, w_out, g_norm_ffn, w_query, sub_keys, expert_u, expert_v, g_out, ts, tq, tb):
    bsz, seq, d = x.shape
    d_lru = conv_w.shape[-1]
    d_sgu = g_v.shape[-1]
    y = _mixer_call(
        x, mod3, g_norm_mix.reshape(1, d), w_in.astype(_BF16), conv_w, conv_b.reshape(1, d_lru),
        w_gate_a.astype(_BF16), b_gate_a.reshape(1, d_lru), w_gate_i.astype(_BF16), b_gate_i.reshape(1, d_lru),
        lru_lambda.reshape(1, d_lru), g_v.reshape(1, d_sgu), w_spatial, b_spatial.T, ts)
    x1, h2, sc = _route_call(
        y, x, mod3, w_out.astype(_BF16), g_norm_ffn.reshape(1, d), w_query.astype(_BF16), sub_keys.astype(_BF16), tq)
    t_total = bsz * seq
    nsel = sub_keys.shape[0] * PEER_TOPK
    tab = _pack_call(expert_u, expert_v, min(256, expert_u.shape[0]))
    out = _peer_call(sc, h2.reshape(t_total, d), x1.reshape(t_total, d), mod3, g_out.reshape(1, d), tab, tb, seq, nsel)
    return out.reshape(bsz, seq, d)


def kernel(x, c, w_ada, b_ada, g_norm_mix, w_in, conv_w, conv_b, w_gate_a, b_gate_a, w_gate_i, b_gate_i,
           lru_lambda, g_v, w_spatial, b_spatial, w_out, g_norm_ffn, w_query, sub_keys, expert_u, expert_v, g_final):
    depth = w_ada.shape[0]
    assert depth == 1, "the final RMSNorm is fused into the layer's last call"
    bsz, seq, d = x.shape
    ts = min(256, seq)
    tq = min(256, seq)
    tb = 16
    n_mod = w_ada.shape[-1]
    mod = _ada_call(c, w_ada[0], b_ada[0], 1024 if n_mod % 1024 == 0 else n_mod)
    mod3 = mod.reshape(bsz, 6, d)
    return _layer(x, mod3, g_norm_mix[0], w_in[0], conv_w[0], conv_b[0], w_gate_a[0], b_gate_a[0], w_gate_i[0],
                  b_gate_i[0], lru_lambda[0], g_v[0], w_spatial[0], b_spatial[0], w_out[0], g_norm_ffn[0],
                  w_query[0], sub_keys[0], expert_u[0], expert_v[0], g_final, ts, tq, tb)
```

```python
import functools

import jax
import jax.numpy as jnp
from jax import lax
from jax.experimental import pallas as pl
from jax.experimental.pallas import tpu as pltpu

_F32 = jnp.float32
_BF16 = jnp.bfloat16
_I32 = jnp.int32

EPS = 1e-6
LRU_C = 8.0
CONV_WIDTH = 4
PEER_TOPK = 16
LANES = 128
SUBLANES = 8
_NEG_INF = float("-inf")
_POS_INF = float("inf")
_MIB = 1024 * 1024
VMEM_BYTES = 64 * _MIB


def _vmem_limit(resident_bytes):
    return int(min(resident_bytes + 24 * _MIB, VMEM_BYTES * 15 // 16))


def _gelu(x):
    return 0.5 * x * (1.0 + jnp.tanh(0.7978845608028654 * (x + 0.044715 * (x * x * x))))


def _softplus(y):
    return jnp.maximum(y, 0.0) + jnp.log1p(jnp.exp(-jnp.abs(y)))


def _rms(x, g):
    return x * lax.rsqrt(jnp.mean(x * x, axis=-1, keepdims=True) + EPS) * g


def _bdot(a, b):
    return jnp.dot(a, b, preferred_element_type=_F32)


def _ada_kernel(c_ref, w_ref, b_ref, o_ref):
    c = c_ref[...]
    a = c * jax.nn.sigmoid(c)
    o_ref[...] = _bdot(a.astype(_BF16), w_ref[...].astype(_BF16)) + b_ref[...]


def _ada_call(c, w, b, tn):
    bsz, d = c.shape
    n = w.shape[1]
    return pl.pallas_call(
        _ada_kernel,
        grid=(n // tn,),
        in_specs=[
            pl.BlockSpec((bsz, d), lambda j: (0, 0)),
            pl.BlockSpec((d, tn), lambda j: (0, j)),
            pl.BlockSpec((1, tn), lambda j: (0, j)),
        ],
        out_specs=pl.BlockSpec((bsz, tn), lambda j: (0, j)),
        out_shape=jax.ShapeDtypeStruct((bsz, n), _F32),
        compiler_params=pltpu.CompilerParams(
            dimension_semantics=("arbitrary",), vmem_limit_bytes=_vmem_limit(2 * d * tn * 4)),
    )(c, w, b.reshape(1, n))


def _mixer_kernel(x_ref, mod_ref, gmix_ref, win_ref, cw_ref, cb_ref, wa_ref, ba_ref, wi_ref, bi_ref,
                  lam_ref, gv_ref, wsp_ref, bspt_ref, o_ref, ext_ref, hprev_ref, *, ts, d_lru, heads, groups, chunk):
    s = pl.program_id(1)

    @pl.when(s == 0)
    def _():
        ext_ref[0:SUBLANES, :] = jnp.zeros((SUBLANES, d_lru), _F32)
        hprev_ref[...] = jnp.zeros_like(hprev_ref)

    x = x_ref[...]
    shift = mod_ref[0:1, :]
    scale = mod_ref[1:2, :]
    h = _rms(x, gmix_ref[...]) * (1.0 + scale) + shift
    hb = h.astype(_BF16)

    x_lru = _bdot(hb, win_ref[:, 0:d_lru])
    ext_ref[SUBLANES:SUBLANES + ts, :] = x_lru
    xc = cb_ref[...] + cw_ref[3:4, :] * x_lru
    for k in range(CONV_WIDTH - 1):
        off = SUBLANES - (CONV_WIDTH - 1) + k
        xc = xc + cw_ref[k:k + 1, :] * ext_ref[off:off + ts, :]
    ext_ref[0:SUBLANES, :] = x_lru[ts - SUBLANES:ts, :]

    xcb = xc.astype(_BF16)
    blk = d_lru // heads
    ra, ia = [], []
    for hh in range(heads):
        xh = xcb[:, hh * blk:(hh + 1) * blk]
        ra.append(_bdot(xh, wa_ref[hh]))
        ia.append(_bdot(xh, wi_ref[hh]))
    r = jax.nn.sigmoid(jnp.concatenate(ra, axis=1) + ba_ref[...])
    ig = jax.nn.sigmoid(jnp.concatenate(ia, axis=1) + bi_ref[...])
    log_a = (-LRU_C) * r * _softplus(-lam_ref[...])
    a = jnp.exp(log_a)
    bv = jnp.sqrt(-jnp.tanh(log_a) * (a * a + 1.0)) * (ig * xc)

    sub = lax.broadcasted_iota(_I32, (ts, d_lru), 0) % SUBLANES
    dist = 1
    while dist < SUBLANES:
        a_s = pltpu.roll(a, dist, 0)
        b_s = pltpu.roll(bv, dist, 0)
        m = sub >= dist
        bv = jnp.where(m, a * b_s + bv, bv)
        a = jnp.where(m, a * a_s, a)
        dist *= 2
    carry = hprev_ref[0:1, :]
    groups_h = []
    for gi in range(ts // SUBLANES):
        hg = bv[gi * SUBLANES:(gi + 1) * SUBLANES, :] + a[gi * SUBLANES:(gi + 1) * SUBLANES, :] * carry
        groups_h.append(hg)
        carry = hg[SUBLANES - 1:SUBLANES, :]
    hs = jnp.concatenate(groups_h, axis=0)
    hprev_ref[0:1, :] = carry

    y_gate = _bdot(hb, win_ref[:, d_lru:2 * d_lru])
    o_ref[:, 0:d_lru] = (hs * _gelu(y_gate)).astype(_BF16)

    d_sgu = gv_ref.shape[1]
    gd = d_sgu // groups
    u = _gelu(_bdot(hb, win_ref[:, 2 * d_lru:2 * d_lru + d_sgu]))
    v = _gelu(_bdot(hb, win_ref[:, 2 * d_lru + d_sgu:2 * d_lru + 2 * d_sgu]))
    vnb = _rms(v, gv_ref[...]).astype(_BF16)
    tri = lax.broadcasted_iota(_I32, (chunk, chunk), 0) >= lax.broadcasted_iota(_I32, (chunk, chunk), 1)
    for g in range(groups):
        wm = jnp.where(tri, wsp_ref[g], 0.0).astype(_BF16)
        bcol = bspt_ref[:, g:g + 1]
        for n in range(ts // chunk):
            sg = _bdot(wm, vnb[n * chunk:(n + 1) * chunk, g * gd:(g + 1) * gd]) + bcol
            ug = u[n * chunk:(n + 1) * chunk, g * gd:(g + 1) * gd]
            o_ref[n * chunk:(n + 1) * chunk, d_lru + g * gd:d_lru + (g + 1) * gd] = (ug * sg).astype(_BF16)


def _mixer_call(x, mod3, g_mix, w_in_b, conv_w, conv_b, w_a_b, b_a, w_i_b, b_i, lam, g_v, w_sp, b_sp_t, ts):
    bsz, seq, d = x.shape
    d_lru = conv_w.shape[1]
    d_sgu = g_v.shape[1]
    heads = w_a_b.shape[0]
    groups, chunk, _ = w_sp.shape
    d_mix = d_lru + d_sgu
    const = lambda shape: pl.BlockSpec(shape, lambda b, s: (0,) * len(shape), pipeline_mode=pl.Buffered(1))
    kern = functools.partial(_mixer_kernel, ts=ts, d_lru=d_lru, heads=heads, groups=groups, chunk=chunk)
    return pl.pallas_call(
        kern,
        grid=(bsz, seq // ts),
        in_specs=[
            pl.BlockSpec((None, ts, d), lambda b, s: (b, s, 0)),
            pl.BlockSpec((None, 6, d), lambda b, s: (b, 0, 0)),
            const((1, d)),
            const(w_in_b.shape),
            const(conv_w.shape),
            const((1, d_lru)),
            const(w_a_b.shape),
            const((1, d_lru)),
            const(w_i_b.shape),
            const((1, d_lru)),
            const((1, d_lru)),
            const((1, d_sgu)),
            const(w_sp.shape),
            const(b_sp_t.shape),
        ],
        out_specs=pl.BlockSpec((None, ts, d_mix), lambda b, s: (b, s, 0)),
        out_shape=jax.ShapeDtypeStruct((bsz, seq, d_mix), _BF16),
        scratch_shapes=[pltpu.VMEM((ts + SUBLANES, d_lru), _F32), pltpu.VMEM((SUBLANES, d_lru), _F32)],
        compiler_params=pltpu.CompilerParams(
            dimension_semantics=("arbitrary", "arbitrary"),
            vmem_limit_bytes=_vmem_limit(w_in_b.size * 2 + 2 * ts * d * 4 + 2 * ts * d_mix * 2 + 12 * ts * d_lru * 4)),
    )(x, mod3, g_mix, w_in_b, conv_w, conv_b, w_a_b, b_a, w_i_b, b_i, lam, g_v, w_sp, b_sp_t)


def _topk_cols(vals, pos, ids, k, tick=None):
    n, t = vals.shape
    kio = lax.broadcasted_iota(_I32, (k, t), 0)
    out_v = jnp.zeros((k, t), _F32)
    out_i = jnp.zeros((k, t), _F32)
    for j in range(k):
        if tick is not None:
            tick()
        m = jnp.max(vals, axis=0, keepdims=True)
        p = jnp.min(jnp.where(vals == m, pos, _POS_INF), axis=0, keepdims=True)
        hit = pos == p
        e = p if ids is None else jnp.max(jnp.where(hit, ids, -1.0), axis=0, keepdims=True)
        out_v = jnp.where(kio == j, m, out_v)
        out_i = jnp.where(kio == j, e, out_i)
        vals = jnp.where(hit, _NEG_INF, vals)
    return out_v, out_i


def _pair_candidates(v1, i1, v2, i2, n_keys):
    k, t = v1.shape
    half = k // 2
    jrow = lax.broadcasted_iota(_I32, (half, t), 0)
    vals = [v1[0:1, :] + v2]
    pos = [lax.broadcasted_iota(_I32, (k, t), 0).astype(_F32)]
    ids = [i1[0:1, :] * n_keys + i2]
    for i in range(1, half):
        keep = jrow < k // (i + 1)
        vals.append(jnp.where(keep, v1[i:i + 1, :] + v2[0:half, :], _NEG_INF))
        pos.append((jrow + i * k).astype(_F32))
        ids.append(i1[i:i + 1, :] * n_keys + i2[0:half, :])
    vals.append(v1[half:k, :] + v2[0:1, :])
    pos.append(((jrow + half) * k).astype(_F32))
    ids.append(i1[half:k, :] * n_keys + i2[0:1, :])
    return jnp.concatenate(vals, axis=0), jnp.concatenate(pos, axis=0), jnp.concatenate(ids, axis=0)


def _route_kernel(y_ref, x_ref, mod_ref, wout_ref, gffn_ref, wq_ref, keys_ref,
                  x1_ref, h2_ref, sc_ref, *, heads):
    x1 = x_ref[...] + mod_ref[2:3, :] * _bdot(y_ref[...], wout_ref[...])
    x1_ref[...] = x1
    h2 = _rms(x1, gffn_ref[...]) * (1.0 + mod_ref[4:5, :]) + mod_ref[3:4, :]
    h2_ref[...] = h2
    q = _bdot(h2.astype(_BF16), wq_ref[...]).astype(_BF16)
    dk = keys_ref.shape[-1]
    nt = (((1,), (1,)), ((), ()))
    for c in range(2 * heads):
        sc_ref[c] = lax.dot_general(keys_ref[c // 2, c % 2], q[:, c * dk:(c + 1) * dk], nt,
                                    preferred_element_type=_F32)


def _route_call(y, x, mod3, w_out_b, g_ffn, w_q_b, keys_b, tq):
    bsz, seq, d = x.shape
    d_mix = y.shape[-1]
    heads, _, n_keys, dk = keys_b.shape
    nq = seq // tq
    const = lambda shape: pl.BlockSpec(shape, lambda b, s: (0,) * len(shape), pipeline_mode=pl.Buffered(1))
    kern = functools.partial(_route_kernel, heads=heads)
    return pl.pallas_call(
        kern,
        grid=(bsz, nq),
        in_specs=[
            pl.BlockSpec((None, tq, d_mix), lambda b, s: (b, s, 0)),
            pl.BlockSpec((None, tq, d), lambda b, s: (b, s, 0)),
            pl.BlockSpec((None, 6, d), lambda b, s: (b, 0, 0)),
            const(w_out_b.shape),
            const((1, d)),
            const(w_q_b.shape),
            const(keys_b.shape),
        ],
        out_specs=[
            pl.BlockSpec((None, tq, d), lambda b, s: (b, s, 0)),
            pl.BlockSpec((None, tq, d), lambda b, s: (b, s, 0)),
            pl.BlockSpec((2 * heads, n_keys, tq), lambda b, s: (0, 0, b * nq + s)),
        ],
        out_shape=[
            jax.ShapeDtypeStruct((bsz, seq, d), _F32),
            jax.ShapeDtypeStruct((bsz, seq, d), _F32),
            jax.ShapeDtypeStruct((2 * heads, n_keys, bsz * seq), _F32),
        ],
        compiler_params=pltpu.CompilerParams(
            dimension_semantics=("arbitrary", "arbitrary"),
            vmem_limit_bytes=_vmem_limit((w_out_b.size + w_q_b.size) * 2 + 2 * tq * (d_mix * 2 + 3 * d * 4)
                                         + 2 * 2 * heads * n_keys * tq * 4)),
    )(y, x, mod3, w_out_b, g_ffn, w_q_b, keys_b)


def _pack_kernel(u_ref, v_ref, o_ref):
    w = pltpu.pack_elementwise([u_ref[...], v_ref[...]], packed_dtype=_BF16)
    for c in range(o_ref.shape[1]):
        o_ref[:, c, :] = w[:, c * LANES:(c + 1) * LANES]


def _pack_call(expert_u, expert_v, rows):
    n, d = expert_u.shape
    spec = pl.BlockSpec((rows, d), lambda i: (i, 0))
    return pl.pallas_call(
        _pack_kernel,
        grid=(n // rows,),
        in_specs=[spec, spec],
        out_specs=pl.BlockSpec((rows, d // LANES, LANES), lambda i: (i, 0, 0)),
        out_shape=jax.ShapeDtypeStruct((n, d // LANES, LANES), jnp.uint32),
        compiler_params=pltpu.CompilerParams(
            dimension_semantics=("arbitrary",), vmem_limit_bytes=_vmem_limit(2 * 3 * rows * d * 4)),
    )(expert_u, expert_v)


def _peer_kernel(sc_ref, h_ref, x1_ref, mod_ref, gfin_ref, tab_ref, tabw_ref, o_ref,
                 buf0_ref, buf1_ref, sem_ref, zero_ref, idxv_ref, gt_ref, gs_ref, idxs_ref, ssem_ref,
                 *, tb, nsel, d, nsb, n_keys):
    s = pl.program_id(0)
    k = pl.program_id(1)
    steps = pl.num_programs(1)
    nct = d // LANES
    tok_tiles = (nsel // SUBLANES) * nct
    bufs = (buf0_ref, buf1_ref)
    sb_tokens = sc_ref.shape[-1]
    nblocks = sb_tokens // tb

    def row_copy(row, sl, t, e):
        first = t * tok_tiles + (e // SUBLANES) * nct
        return pltpu.make_async_copy(
            tab_ref.at[row], bufs[sl].at[pl.ds(first, nct), e % SUBLANES, :], sem_ref.at[sl, t])

    def token_wait(sl, t):
        pltpu.make_async_copy(
            tabw_ref.at[pl.ds(0, tok_tiles)], bufs[sl].at[pl.ds(t * tok_tiles, tok_tiles)], sem_ref.at[sl, t]).wait()

    key_pos = lax.broadcasted_iota(_I32, (n_keys, sb_tokens), 0).astype(_F32)

    def route_head(hl, tick):
        tops = [_topk_cols(sc_ref[2 * hl + p], key_pos, None, PEER_TOPK, tick) for p in range(2)]
        (v1, i1), (v2, i2) = tops
        cand, cpos, cids = _pair_candidates(v1, i1, v2, i2, float(n_keys))
        tv, ti = _topk_cols(cand, cpos, cids, PEER_TOPK, tick)
        ex = jnp.exp(tv - tv[0:1, :])
        r = pl.multiple_of((2 * k + hl) * PEER_TOPK, PEER_TOPK)
        idxv_ref[pl.ds(r, PEER_TOPK), :] = ti.astype(_I32)
        gt_ref[pl.ds(r, PEER_TOPK), :] = ex / jnp.sum(ex, axis=0, keepdims=True)

    def ids_to_smem(slot):
        return pltpu.make_async_copy(idxv_ref, idxs_ref.at[slot], ssem_ref)

    def publish():
        @pl.when(k == steps - 1)
        def _():
            gs_ref[s % 3] = gt_ref[...].T
            ids_to_smem(s % 3).start()

    def published():
        @pl.when(jnp.logical_and(k == 0, s > 0))
        def _():
            ids_to_smem((s - 1) % 3).wait()

    ngrp = nsel // SUBLANES
    rows8 = lax.broadcasted_iota(_I32, (SUBLANES, 2 * nsel), 0)
    lane = lax.broadcasted_iota(_I32, (nsel, LANES), 1)
    spread = (lax.broadcasted_iota(_I32, (nsel, 2 * nsel), 1)
              == 2 * lax.broadcasted_iota(_I32, (nsel, 2 * nsel), 0) + 1).astype(_BF16)
    final = jnp.logical_and(s == nsb + 1, k == steps - 1)

    def token_matrix(sl, t):
        w = bufs[sl][t * tok_tiles:(t + 1) * tok_tiles]
        rows = [jnp.concatenate([w[g * nct + c] for c in range(nct)], axis=1) for g in range(ngrp)]
        return pltpu.bitcast(jnp.concatenate(rows, axis=0), _BF16)

    def u_dots(sl, t):
        acc = [jnp.zeros((SUBLANES, LANES), _F32) for _ in range(ngrp)]
        for c in range(nct):
            xc = h_ref[sl * tb + t:sl * tb + t + 1, c * LANES:(c + 1) * LANES] + zero_ref[...]
            for g in range(ngrp):
                w = bufs[sl][t * tok_tiles + g * nct + c]
                u = pltpu.unpack_elementwise(w, index=0, packed_dtype=_BF16, unpacked_dtype=_F32)
                acc[g] = acc[g] + u * xc
        return jnp.sum(jnp.concatenate(acc, axis=0), axis=1, keepdims=True)

    def evaluate(sl):
        nsl = 1 - sl
        blk = 2 * k + sl
        nxt = jnp.where(final, blk, blk + 1) if sl == 1 else blk + 1
        nslot = (s - 2 + nxt // nblocks) % 3
        ntok0 = (nxt % nblocks) * tb
        row0 = sl * tb

        def next_row(t, e):
            return idxs_ref[nslot, e, ntok0 + t]
        ngroups = tb // SUBLANES
        copies = [(t, e) for t in range(tb) for e in range(nsel)]
        nticks = 3 * PEER_TOPK
        per_tick = len(copies) // (6 * nticks)
        issued = [0]

        def issue(n):
            for t, e in copies[issued[0]:issued[0] + n]:
                row_copy(next_row(t, e), nsl, t, e).start(priority=e % 2)
            issued[0] += n

        token_wait(sl, 0)
        if sl == 0:
            published()
        route_head(sl, lambda: issue(per_tick))
        if sl == 1:
            publish()
        nregions = (ngroups + 1) * SUBLANES
        share = -(-(len(copies) - issued[0]) // nregions)
        gslot = (s - 2) % 3
        act2 = None
        for p in range(ngroups + 1):
            r0, v0 = p * SUBLANES, (p - 1) * SUBLANES
            zcols = jnp.zeros((nsel, LANES), _F32)
            out = jnp.zeros((SUBLANES, d), _F32)
            for j in range(SUBLANES):
                if p < ngroups and r0 + j > 0:
                    token_wait(sl, r0 + j)
                issue(share)
                if p > 0:
                    aj = jnp.where(rows8 == j, act2, 0.0).astype(_BF16)
                    out = out + _bdot(aj, token_matrix(sl, v0 + j))
                if p < ngroups:
                    zcols = jnp.where(lane == j, u_dots(sl, r0 + j), zcols)
            if p > 0:
                x2 = x1_ref[row0 + v0:row0 + v0 + SUBLANES, :] + mod_ref[5:6, :] * out
                o_ref[row0 + v0:row0 + v0 + SUBLANES, :] = _rms(x2, gfin_ref[...])
            if p < ngroups:
                z = zcols.T[0:SUBLANES, :]
                g_rows = gs_ref[gslot, pl.ds(pl.multiple_of(blk * tb + r0, SUBLANES), SUBLANES), :]
                act = (_gelu(z) * g_rows).astype(_BF16)
                act2 = _bdot(act, spread)

    @pl.when(s < 2)
    def _():
        @pl.when(jnp.logical_and(s == 0, k == 0))
        def _():
            zero_ref[...] = jnp.zeros_like(zero_ref)

        published()
        route_head(0, None)
        route_head(1, None)
        publish()

        @pl.when(jnp.logical_and(s == 1, k == steps - 1))
        def _():
            def body(t, c):
                for e in range(nsel):
                    row_copy(idxs_ref[0, e, t], 0, t, e).start(priority=e % 2)
                return c
            lax.fori_loop(0, tb, body, 0)

    @pl.when(s >= 2)
    def _():
        evaluate(0)
        evaluate(1)

    @pl.when(final)
    def _():
        ids_to_smem(s % 3).wait()
        for t in range(tb):
            token_wait(0, t)


def _peer_call(sc, h2, x1, mod3, g_final, tab3, tb, seq, nsel):
    nhh, n_keys, t_total = sc.shape
    d = h2.shape[-1]
    sb_tokens = LANES
    nsteps = sb_tokens // (2 * tb)
    assert nhh == 4 * nsteps, "two routing heads per evaluation step"
    nsb = t_total // sb_tokens
    nct = d // LANES
    n_exp = tab3.shape[0]
    tabw = tab3.reshape(n_exp * nct // SUBLANES, SUBLANES, LANES)
    slot_tiles = tb * (nsel // SUBLANES) * nct
    blk = lambda s, k: jnp.maximum((s - 2) * nsteps + k, 0)
    kern = functools.partial(_peer_kernel, tb=tb, nsel=nsel, d=d, nsb=nsb, n_keys=n_keys)
    return pl.pallas_call(
        kern,
        grid=(nsb + 2, nsteps),
        in_specs=[
            pl.BlockSpec((4, n_keys, sb_tokens), lambda s, k: (k, 0, jnp.minimum(s, nsb - 1))),
            pl.BlockSpec((2 * tb, d), lambda s, k: (blk(s, k), 0)),
            pl.BlockSpec((2 * tb, d), lambda s, k: (blk(s, k), 0)),
            pl.BlockSpec((None, 6, d), lambda s, k: ((blk(s, k) * 2 * tb) // seq, 0, 0)),
            pl.BlockSpec((1, d), lambda s, k: (0, 0)),
            pl.BlockSpec(memory_space=pl.ANY),
            pl.BlockSpec(memory_space=pl.ANY),
        ],
        out_specs=pl.BlockSpec((2 * tb, d), lambda s, k: (blk(s, k), 0)),
        out_shape=jax.ShapeDtypeStruct((t_total, d), _F32),
        scratch_shapes=[
            pltpu.VMEM((slot_tiles, SUBLANES, LANES), jnp.uint32),
            pltpu.VMEM((slot_tiles, SUBLANES, LANES), jnp.uint32),
            pltpu.SemaphoreType.DMA((2, tb)),
            pltpu.VMEM((SUBLANES, LANES), _F32),
            pltpu.VMEM((nsel, sb_tokens), _I32),
            pltpu.VMEM((nsel, sb_tokens), _F32),
            pltpu.VMEM((3, sb_tokens, nsel), _F32),
            pltpu.SMEM((3, nsel, sb_tokens), _I32),
            pltpu.SemaphoreType.DMA(()),
        ],
        compiler_params=pltpu.CompilerParams(
            dimension_semantics=("arbitrary", "arbitrary"),
            vmem_limit_bytes=_vmem_limit(2 * slot_tiles * SUBLANES * LANES * 4 + 2 * 3 * 2 * tb * d * 4)),
    )(sc, h2, x1, mod3, g_final, tab3, tabw)


def _layer(x, mod3, g_norm_mix, w_in, conv_w, conv_b, w_gate_a, b_gate_a, w_gate_i, b_gate_i, lru_lambda, g_v,
           w_spatial, b_spatial, w_out, g_norm_ffn, w_query, sub_keys, expert_u, expert_v, g_out, ts, tq, tb):
    bsz, seq, d = x.shape
    d_lru = conv_w.shape[-1]
    d_sgu = g_v.shape[-1]
    y = _mixer_call(
        x, mod3, g_norm_mix.reshape(1, d), w_in.astype(_BF16), conv_w, conv_b.reshape(1, d_lru),
        w_gate_a.astype(_BF16), b_gate_a.reshape(1, d_lru), w_gate_i.astype(_BF16), b_gate_i.reshape(1, d_lru),
        lru_lambda.reshape(1, d_lru), g_v.reshape(1, d_sgu), w_spatial, b_spatial.T, ts)
    x1, h2, sc = _route_call(
        y, x, mod3, w_out.astype(_BF16), g_norm_ffn.reshape(1, d), w_query.astype(_BF16), sub_keys.astype(_BF16), tq)
    t_total = bsz * seq
    nsel = sub_keys.shape[0] * PEER_TOPK
    tab = _pack_call(expert_u, expert_v, min(512, expert_u.shape[0]))
    out = _peer_call(sc, h2.reshape(t_total, d), x1.reshape(t_total, d), mod3, g_out.reshape(1, d), tab, tb, seq, nsel)
    return out.reshape(bsz, seq, d)


def kernel(x, c, w_ada, b_ada, g_norm_mix, w_in, conv_w, conv_b, w_gate_a, b_gate_a, w_gate_i, b_gate_i,
           lru_lambda, g_v, w_spatial, b_spatial, w_out, g_norm_ffn, w_query, sub_keys, expert_u, expert_v, g_final):
    depth = w_ada.shape[0]
    assert depth == 1, "the final RMSNorm is fused into the layer's last call"
    bsz, seq, d = x.shape
    ts = min(256, seq)
    tq = min(512, seq)
    tb = 16
    n_mod = w_ada.shape[-1]
    mod = _ada_call(c, w_ada[0], b_ada[0], 1024 if n_mod % 1024 == 0 else n_mod)
    mod3 = mod.reshape(bsz, 6, d)
    return _layer(x, mod3, g_norm_mix[0], w_in[0], conv_w[0], conv_b[0], w_gate_a[0], b_gate_a[0], w_gate_i[0],
                  b_gate_i[0], lru_lambda[0], g_v[0], w_spatial[0], b_spatial[0], w_out[0], g_norm_ffn[0],
                  w_query[0], sub_keys[0], expert_u[0], expert_v[0], g_final, ts, tq, tb)
```

```python
import functools

import jax
import jax.numpy as jnp
from jax import lax
from jax.experimental import pallas as pl
from jax.experimental.pallas import tpu as pltpu

_F32 = jnp.float32
_BF16 = jnp.bfloat16
_I32 = jnp.int32

EPS = 1e-6
LRU_C = 8.0
CONV_WIDTH = 4
PEER_TOPK = 16
LANES = 128
SUBLANES = 8
_NEG_INF = float("-inf")
_POS_INF = float("inf")
_MIB = 1024 * 1024
VMEM_BYTES = 64 * _MIB


def _vmem_limit(resident_bytes):
    return int(min(resident_bytes + 24 * _MIB, VMEM_BYTES * 15 // 16))


def _gelu(x):
    return 0.5 * x * (1.0 + jnp.tanh(0.7978845608028654 * (x + 0.044715 * (x * x * x))))


def _softplus(y):
    return jnp.maximum(y, 0.0) + jnp.log1p(jnp.exp(-jnp.abs(y)))


def _rms(x, g):
    return x * lax.rsqrt(jnp.mean(x * x, axis=-1, keepdims=True) + EPS) * g


def _bdot(a, b):
    return jnp.dot(a, b, preferred_element_type=_F32)


def _ada_kernel(c_ref, w_ref, b_ref, o_ref):
    c = c_ref[...]
    a = c * jax.nn.sigmoid(c)
    o_ref[...] = _bdot(a.astype(_BF16), w_ref[...].astype(_BF16)) + b_ref[...]


def _ada_call(c, w, b, tn):
    bsz, d = c.shape
    n = w.shape[1]
    return pl.pallas_call(
        _ada_kernel,
        grid=(n // tn,),
        in_specs=[
            pl.BlockSpec((bsz, d), lambda j: (0, 0)),
            pl.BlockSpec((d, tn), lambda j: (0, j)),
            pl.BlockSpec((1, tn), lambda j: (0, j)),
        ],
        out_specs=pl.BlockSpec((bsz, tn), lambda j: (0, j)),
        out_shape=jax.ShapeDtypeStruct((bsz, n), _F32),
        compiler_params=pltpu.CompilerParams(
            dimension_semantics=("arbitrary",), vmem_limit_bytes=_vmem_limit(2 * d * tn * 4)),
    )(c, w, b.reshape(1, n))


def _mixer_kernel(x_ref, mod_ref, gmix_ref, win_ref, cw_ref, cb_ref, wa_ref, ba_ref, wi_ref, bi_ref,
                  lam_ref, gv_ref, wsp_ref, bspt_ref, o_ref, ext_ref, hprev_ref, *, ts, d_lru, heads, groups, chunk):
    s = pl.program_id(1)

    @pl.when(s == 0)
    def _():
        ext_ref[0:SUBLANES, :] = jnp.zeros((SUBLANES, d_lru), _F32)
        hprev_ref[...] = jnp.zeros_like(hprev_ref)

    x = x_ref[...]
    shift = mod_ref[0:1, :]
    scale = mod_ref[1:2, :]
    h = _rms(x, gmix_ref[...]) * (1.0 + scale) + shift
    hb = h.astype(_BF16)

    x_lru = _bdot(hb, win_ref[:, 0:d_lru])
    ext_ref[SUBLANES:SUBLANES + ts, :] = x_lru
    xc = cb_ref[...] + cw_ref[3:4, :] * x_lru
    for k in range(CONV_WIDTH - 1):
        off = SUBLANES - (CONV_WIDTH - 1) + k
        xc = xc + cw_ref[k:k + 1, :] * ext_ref[off:off + ts, :]
    ext_ref[0:SUBLANES, :] = x_lru[ts - SUBLANES:ts, :]

    xcb = xc.astype(_BF16)
    blk = d_lru // heads
    ra, ia = [], []
    for hh in range(heads):
        xh = xcb[:, hh * blk:(hh + 1) * blk]
        ra.append(_bdot(xh, wa_ref[hh]))
        ia.append(_bdot(xh, wi_ref[hh]))
    r = jax.nn.sigmoid(jnp.concatenate(ra, axis=1) + ba_ref[...])
    ig = jax.nn.sigmoid(jnp.concatenate(ia, axis=1) + bi_ref[...])
    log_a = (-LRU_C) * r * _softplus(-lam_ref[...])
    a = jnp.exp(log_a)
    bv = jnp.sqrt(-jnp.tanh(log_a) * (a * a + 1.0)) * (ig * xc)

    sub = lax.broadcasted_iota(_I32, (ts, d_lru), 0) % SUBLANES
    dist = 1
    while dist < SUBLANES:
        a_s = pltpu.roll(a, dist, 0)
        b_s = pltpu.roll(bv, dist, 0)
        m = sub >= dist
        bv = jnp.where(m, a * b_s + bv, bv)
        a = jnp.where(m, a * a_s, a)
        dist *= 2
    carry = hprev_ref[0:1, :]
    groups_h = []
    for gi in range(ts // SUBLANES):
        hg = bv[gi * SUBLANES:(gi + 1) * SUBLANES, :] + a[gi * SUBLANES:(gi + 1) * SUBLANES, :] * carry
        groups_h.append(hg)
        carry = hg[SUBLANES - 1:SUBLANES, :]
    hs = jnp.concatenate(groups_h, axis=0)
    hprev_ref[0:1, :] = carry

    y_gate = _bdot(hb, win_ref[:, d_lru:2 * d_lru])
    o_ref[:, 0:d_lru] = (hs * _gelu(y_gate)).astype(_BF16)

    d_sgu = gv_ref.shape[1]
    gd = d_sgu // groups
    u = _gelu(_bdot(hb, win_ref[:, 2 * d_lru:2 * d_lru + d_sgu]))
    v = _gelu(_bdot(hb, win_ref[:, 2 * d_lru + d_sgu:2 * d_lru + 2 * d_sgu]))
    vnb = _rms(v, gv_ref[...]).astype(_BF16)
    tri = lax.broadcasted_iota(_I32, (chunk, chunk), 0) >= lax.broadcasted_iota(_I32, (chunk, chunk), 1)
    for g in range(groups):
        wm = jnp.where(tri, wsp_ref[g], 0.0).astype(_BF16)
        bcol = bspt_ref[:, g:g + 1]
        for n in range(ts // chunk):
            sg = _bdot(wm, vnb[n * chunk:(n + 1) * chunk, g * gd:(g + 1) * gd]) + bcol
            ug = u[n * chunk:(n + 1) * chunk, g * gd:(g + 1) * gd]
            o_ref[n * chunk:(n + 1) * chunk, d_lru + g * gd:d_lru + (g + 1) * gd] = (ug * sg).astype(_BF16)


def _mixer_call(x, mod3, g_mix, w_in_b, conv_w, conv_b, w_a_b, b_a, w_i_b, b_i, lam, g_v, w_sp, b_sp_t, ts):
    bsz, seq, d = x.shape
    d_lru = conv_w.shape[1]
    d_sgu = g_v.shape[1]
    heads = w_a_b.shape[0]
    groups, chunk, _ = w_sp.shape
    d_mix = d_lru + d_sgu
    const = lambda shape: pl.BlockSpec(shape, lambda b, s: (0,) * len(shape), pipeline_mode=pl.Buffered(1))
    kern = functools.partial(_mixer_kernel, ts=ts, d_lru=d_lru, heads=heads, groups=groups, chunk=chunk)
    return pl.pallas_call(
        kern,
        grid=(bsz, seq // ts),
        in_specs=[
            pl.BlockSpec((None, ts, d), lambda b, s: (b, s, 0)),
            pl.BlockSpec((None, 6, d), lambda b, s: (b, 0, 0)),
            const((1, d)),
            const(w_in_b.shape),
            const(conv_w.shape),
            const((1, d_lru)),
            const(w_a_b.shape),
            const((1, d_lru)),
            const(w_i_b.shape),
            const((1, d_lru)),
            const((1, d_lru)),
            const((1, d_sgu)),
            const(w_sp.shape),
            const(b_sp_t.shape),
        ],
        out_specs=pl.BlockSpec((None, ts, d_mix), lambda b, s: (b, s, 0)),
        out_shape=jax.ShapeDtypeStruct((bsz, seq, d_mix), _BF16),
        scratch_shapes=[pltpu.VMEM((ts + SUBLANES, d_lru), _F32), pltpu.VMEM((SUBLANES, d_lru), _F32)],
        compiler_params=pltpu.CompilerParams(
            dimension_semantics=("arbitrary", "arbitrary"),
            vmem_limit_bytes=_vmem_limit(w_in_b.size * 2 + 2 * ts * d * 4 + 2 * ts * d_mix * 2 + 12 * ts * d_lru * 4)),
    )(x, mod3, g_mix, w_in_b, conv_w, conv_b, w_a_b, b_a, w_i_b, b_i, lam, g_v, w_sp, b_sp_t)


def _topk_cols(vals, pos, ids, k, tick=None):
    n, t = vals.shape
    kio = lax.broadcasted_iota(_I32, (k, t), 0)
    out_v = jnp.zeros((k, t), _F32)
    out_i = jnp.zeros((k, t), _F32)
    for j in range(k):
        if tick is not None:
            tick()
        m = jnp.max(vals, axis=0, keepdims=True)
        p = jnp.min(jnp.where(vals == m, pos, _POS_INF), axis=0, keepdims=True)
        hit = pos == p
        e = p if ids is None else jnp.max(jnp.where(hit, ids, -1.0), axis=0, keepdims=True)
        out_v = jnp.where(kio == j, m, out_v)
        out_i = jnp.where(kio == j, e, out_i)
        vals = jnp.where(hit, _NEG_INF, vals)
    return out_v, out_i


def _pair_candidates(v1, i1, v2, i2, n_keys):
    k, t = v1.shape
    half = k // 2
    jrow = lax.broadcasted_iota(_I32, (half, t), 0)
    vals = [v1[0:1, :] + v2]
    pos = [lax.broadcasted_iota(_I32, (k, t), 0).astype(_F32)]
    ids = [i1[0:1, :] * n_keys + i2]
    for i in range(1, half):
        keep = jrow < k // (i + 1)
        vals.append(jnp.where(keep, v1[i:i + 1, :] + v2[0:half, :], _NEG_INF))
        pos.append((jrow + i * k).astype(_F32))
        ids.append(i1[i:i + 1, :] * n_keys + i2[0:half, :])
    vals.append(v1[half:k, :] + v2[0:1, :])
    pos.append(((jrow + half) * k).astype(_F32))
    ids.append(i1[half:k, :] * n_keys + i2[0:1, :])
    return jnp.concatenate(vals, axis=0), jnp.concatenate(pos, axis=0), jnp.concatenate(ids, axis=0)


def _route_kernel(y_ref, x_ref, mod_ref, wout_ref, gffn_ref, wq_ref, keys_ref,
                  x1_ref, h2_ref, sc_ref, *, heads):
    x1 = x_ref[...] + mod_ref[2:3, :] * _bdot(y_ref[...], wout_ref[...])
    x1_ref[...] = x1
    h2 = _rms(x1, gffn_ref[...]) * (1.0 + mod_ref[4:5, :]) + mod_ref[3:4, :]
    h2_ref[...] = h2
    q = _bdot(h2.astype(_BF16), wq_ref[...]).astype(_BF16)
    dk = keys_ref.shape[-1]
    nt = (((1,), (1,)), ((), ()))
    for c in range(2 * heads):
        sc_ref[c] = lax.dot_general(keys_ref[c // 2, c % 2], q[:, c * dk:(c + 1) * dk], nt,
                                    preferred_element_type=_F32)


def _route_call(y, x, mod3, w_out_b, g_ffn, w_q_b, keys_b, tq):
    bsz, seq, d = x.shape
    d_mix = y.shape[-1]
    heads, _, n_keys, dk = keys_b.shape
    nq = seq // tq
    const = lambda shape: pl.BlockSpec(shape, lambda b, s: (0,) * len(shape), pipeline_mode=pl.Buffered(1))
    kern = functools.partial(_route_kernel, heads=heads)
    return pl.pallas_call(
        kern,
        grid=(bsz, nq),
        in_specs=[
            pl.BlockSpec((None, tq, d_mix), lambda b, s: (b, s, 0)),
            pl.BlockSpec((None, tq, d), lambda b, s: (b, s, 0)),
            pl.BlockSpec((None, 6, d), lambda b, s: (b, 0, 0)),
            const(w_out_b.shape),
            const((1, d)),
            const(w_q_b.shape),
            const(keys_b.shape),
        ],
        out_specs=[
            pl.BlockSpec((None, tq, d), lambda b, s: (b, s, 0)),
            pl.BlockSpec((None, tq, d), lambda b, s: (b, s, 0)),
            pl.BlockSpec((2 * heads, n_keys, tq), lambda b, s: (0, 0, b * nq + s)),
        ],
        out_shape=[
            jax.ShapeDtypeStruct((bsz, seq, d), _F32),
            jax.ShapeDtypeStruct((bsz, seq, d), _F32),
            jax.ShapeDtypeStruct((2 * heads, n_keys, bsz * seq), _F32),
        ],
        compiler_params=pltpu.CompilerParams(
            dimension_semantics=("arbitrary", "arbitrary"),
            vmem_limit_bytes=_vmem_limit((w_out_b.size + w_q_b.size) * 2 + 2 * tq * (d_mix * 2 + 3 * d * 4)
                                         + 2 * 2 * heads * n_keys * tq * 4)),
    )(y, x, mod3, w_out_b, g_ffn, w_q_b, keys_b)


def _pack_kernel(u_ref, v_ref, o_ref):
    w = pltpu.pack_elementwise([u_ref[...], v_ref[...]], packed_dtype=_BF16)
    for c in range(o_ref.shape[1]):
        o_ref[:, c, :] = w[:, c * LANES:(c + 1) * LANES]


def _pack_call(expert_u, expert_v, rows):
    n, d = expert_u.shape
    spec = pl.BlockSpec((rows, d), lambda i: (i, 0))
    return pl.pallas_call(
        _pack_kernel,
        grid=(n // rows,),
        in_specs=[spec, spec],
        out_specs=pl.BlockSpec((rows, d // LANES, LANES), lambda i: (i, 0, 0)),
        out_shape=jax.ShapeDtypeStruct((n, d // LANES, LANES), jnp.uint32),
        compiler_params=pltpu.CompilerParams(
            dimension_semantics=("arbitrary",), vmem_limit_bytes=_vmem_limit(2 * 3 * rows * d * 4)),
    )(expert_u, expert_v)


def _peer_kernel(sc_ref, h_ref, x1_ref, mod_ref, gfin_ref, tab_ref, tabw_ref, o_ref,
                 buf0_ref, buf1_ref, sem_ref, zero_ref, idxv_ref, gt_ref, gs_ref, idxs_ref, ssem_ref,
                 *, tb, nsel, d, nsb, n_keys):
    s = pl.program_id(0)
    k = pl.program_id(1)
    steps = pl.num_programs(1)
    nct = d // LANES
    tok_tiles = (nsel // SUBLANES) * nct
    bufs = (buf0_ref, buf1_ref)
    sb_tokens = sc_ref.shape[-1]
    nblocks = sb_tokens // tb

    def row_copy(row, sl, t, e):
        first = t * tok_tiles + (e // SUBLANES) * nct
        return pltpu.make_async_copy(
            tab_ref.at[row], bufs[sl].at[pl.ds(first, nct), e % SUBLANES, :], sem_ref.at[sl, t])

    def token_wait(sl, t):
        pltpu.make_async_copy(
            tabw_ref.at[pl.ds(0, tok_tiles)], bufs[sl].at[pl.ds(t * tok_tiles, tok_tiles)], sem_ref.at[sl, t]).wait()

    key_pos = lax.broadcasted_iota(_I32, (n_keys, sb_tokens), 0).astype(_F32)

    def route_head(hl, tick):
        tops = [_topk_cols(sc_ref[2 * hl + p], key_pos, None, PEER_TOPK, tick) for p in range(2)]
        (v1, i1), (v2, i2) = tops
        cand, cpos, cids = _pair_candidates(v1, i1, v2, i2, float(n_keys))
        tv, ti = _topk_cols(cand, cpos, cids, PEER_TOPK, tick)
        ex = jnp.exp(tv - tv[0:1, :])
        r = pl.multiple_of((2 * k + hl) * PEER_TOPK, PEER_TOPK)
        idxv_ref[pl.ds(r, PEER_TOPK), :] = ti.astype(_I32)
        gt_ref[pl.ds(r, PEER_TOPK), :] = ex / jnp.sum(ex, axis=0, keepdims=True)

    def ids_to_smem(slot):
        return pltpu.make_async_copy(idxv_ref, idxs_ref.at[slot], ssem_ref)

    def publish():
        @pl.when(k == steps - 1)
        def _():
            gs_ref[s % 3] = gt_ref[...].T
            ids_to_smem(s % 3).start()

    def published():
        @pl.when(jnp.logical_and(k == 0, s > 0))
        def _():
            ids_to_smem((s - 1) % 3).wait()

    ngrp = nsel // SUBLANES
    rows8 = lax.broadcasted_iota(_I32, (SUBLANES, 2 * nsel), 0)
    lane = lax.broadcasted_iota(_I32, (nsel, LANES), 1)
    spread = (lax.broadcasted_iota(_I32, (nsel, 2 * nsel), 1)
              == 2 * lax.broadcasted_iota(_I32, (nsel, 2 * nsel), 0) + 1).astype(_BF16)
    final = jnp.logical_and(s == nsb + 1, k == steps - 1)

    def token_matrix(sl, t):
        w = bufs[sl][t * tok_tiles:(t + 1) * tok_tiles]
        rows = [jnp.concatenate([w[g * nct + c] for c in range(nct)], axis=1) for g in range(ngrp)]
        return pltpu.bitcast(jnp.concatenate(rows, axis=0), _BF16)

    def u_dots(sl, t):
        acc = [jnp.zeros((SUBLANES, LANES), _F32) for _ in range(ngrp)]
        for c in range(nct):
            xc = h_ref[sl * tb + t:sl * tb + t + 1, c * LANES:(c + 1) * LANES] + zero_ref[...]
            for g in range(ngrp):
                w = bufs[sl][t * tok_tiles + g * nct + c]
                u = pltpu.unpack_elementwise(w, index=0, packed_dtype=_BF16, unpacked_dtype=_F32)
                acc[g] = acc[g] + u * xc
        return jnp.sum(jnp.concatenate(acc, axis=0), axis=1, keepdims=True)

    def evaluate(sl):
        nsl = 1 - sl
        blk = 2 * k + sl
        nxt = jnp.where(final, blk, blk + 1) if sl == 1 else blk + 1
        nslot = (s - 2 + nxt // nblocks) % 3
        ntok0 = (nxt % nblocks) * tb
        row0 = sl * tb

        def next_row(t, e):
            return idxs_ref[nslot, e, ntok0 + t]
        ngroups = tb // SUBLANES
        copies = [(t, e) for t in range(tb) for e in range(nsel)]
        nticks = 3 * PEER_TOPK
        per_tick = len(copies) // (6 * nticks)
        issued = [0]

        def issue(n):
            for t, e in copies[issued[0]:issued[0] + n]:
                row_copy(next_row(t, e), nsl, t, e).start(priority=e % 2)
            issued[0] += n

        token_wait(sl, 0)
        if sl == 0:
            published()
        route_head(sl, lambda: issue(per_tick))
        if sl == 1:
            publish()
        nregions = (ngroups + 1) * SUBLANES
        share = -(-(len(copies) - issued[0]) // nregions)
        gslot = (s - 2) % 3
        act2 = None
        for p in range(ngroups + 1):
            r0, v0 = p * SUBLANES, (p - 1) * SUBLANES
            zcols = jnp.zeros((nsel, LANES), _F32)
            out = jnp.zeros((SUBLANES, d), _F32)
            for j in range(SUBLANES):
                if p < ngroups and r0 + j > 0:
                    token_wait(sl, r0 + j)
                issue(share)
                if p > 0:
                    aj = jnp.where(rows8 == j, act2, 0.0).astype(_BF16)
                    out = out + _bdot(aj, token_matrix(sl, v0 + j))
                if p < ngroups:
                    zcols = jnp.where(lane == j, u_dots(sl, r0 + j), zcols)
            if p > 0:
                x2 = x1_ref[row0 + v0:row0 + v0 + SUBLANES, :] + mod_ref[5:6, :] * out
                o_ref[row0 + v0:row0 + v0 + SUBLANES, :] = _rms(x2, gfin_ref[...])
            if p < ngroups:
                z = zcols.T[0:SUBLANES, :]
                g_rows = gs_ref[gslot, pl.ds(pl.multiple_of(blk * tb + r0, SUBLANES), SUBLANES), :]
                act = (_gelu(z) * g_rows).astype(_BF16)
                act2 = _bdot(act, spread)

    @pl.when(s < 2)
    def _():
        @pl.when(jnp.logical_and(s == 0, k == 0))
        def _():
            zero_ref[...] = jnp.zeros_like(zero_ref)

        published()
        route_head(0, None)
        route_head(1, None)
        publish()

        @pl.when(jnp.logical_and(s == 1, k == steps - 1))
        def _():
            def body(t, c):
                for e in range(nsel):
                    row_copy(idxs_ref[0, e, t], 0, t, e).start(priority=e % 2)
                return c
            lax.fori_loop(0, tb, body, 0)

    @pl.when(s >= 2)
    def _():
        evaluate(0)
        evaluate(1)

    @pl.when(final)
    def _():
        ids_to_smem(s % 3).wait()
        for t in range(tb):
            token_wait(0, t)


def _peer_call(sc, h2, x1, mod3, g_final, tab3, tb, seq, nsel):
    nhh, n_keys, t_total = sc.shape
    d = h2.shape[-1]
    sb_tokens = LANES
    nsteps = sb_tokens // (2 * tb)
    assert nhh == 4 * nsteps, "two routing heads per evaluation step"
    nsb = t_total // sb_tokens
    nct = d // LANES
    n_exp = tab3.shape[0]
    tabw = tab3.reshape(n_exp * nct // SUBLANES, SUBLANES, LANES)
    slot_tiles = tb * (nsel // SUBLANES) * nct
    blk = lambda s, k: jnp.maximum((s - 2) * nsteps + k, 0)
    kern = functools.partial(_peer_kernel, tb=tb, nsel=nsel, d=d, nsb=nsb, n_keys=n_keys)
    return pl.pallas_call(
        kern,
        grid=(nsb + 2, nsteps),
        in_specs=[
            pl.BlockSpec((4, n_keys, sb_tokens), lambda s, k: (k, 0, jnp.minimum(s, nsb - 1))),
            pl.BlockSpec((2 * tb, d), lambda s, k: (blk(s, k), 0)),
            pl.BlockSpec((2 * tb, d), lambda s, k: (blk(s, k), 0)),
            pl.BlockSpec((None, 6, d), lambda s, k: ((blk(s, k) * 2 * tb) // seq, 0, 0)),
            pl.BlockSpec((1, d), lambda s, k: (0, 0)),
            pl.BlockSpec(memory_space=pl.ANY),
            pl.BlockSpec(memory_space=pl.ANY),
        ],
        out_specs=pl.BlockSpec((2 * tb, d), lambda s, k: (blk(s, k), 0)),
        out_shape=jax.ShapeDtypeStruct((t_total, d), _F32),
        scratch_shapes=[
            pltpu.VMEM((slot_tiles, SUBLANES, LANES), jnp.uint32),
            pltpu.VMEM((slot_tiles, SUBLANES, LANES), jnp.uint32),
            pltpu.SemaphoreType.DMA((2, tb)),
            pltpu.VMEM((SUBLANES, LANES), _F32),
            pltpu.VMEM((nsel, sb_tokens), _I32),
            pltpu.VMEM((nsel, sb_tokens), _F32),
            pltpu.VMEM((3, sb_tokens, nsel), _F32),
            pltpu.SMEM((3, nsel, sb_tokens), _I32),
            pltpu.SemaphoreType.DMA(()),
        ],
        compiler_params=pltpu.CompilerParams(
            dimension_semantics=("arbitrary", "arbitrary"),
            vmem_limit_bytes=_vmem_limit(2 * slot_tiles * SUBLANES * LANES * 4 + 2 * 3 * 2 * tb * d * 4)),
    )(sc, h2, x1, mod3, g_final, tab3, tabw)


def _layer(x, mod3, g_norm_mix, w_in, conv_w, conv_b, w_gate_a, b_gate_a, w_gate_i, b_gate_i, lru_lambda, g_v,
           w_spatial, b_spatial, w_out, g_norm_ffn, w_query, sub_keys, expert_u, expert_v, g_out, ts, tq, tb):
    bsz, seq, d = x.shape
    d_lru = conv_w.shape[-1]
    d_sgu = g_v.shape[-1]
    y = _mixer_call(
        x, mod3, g_norm_mix.reshape(1, d), w_in.astype(_BF16), conv_w, conv_b.reshape(1, d_lru),
        w_gate_a.astype(_BF16), b_gate_a.reshape(1, d_lru), w_gate_i.astype(_BF16), b_gate_i.reshape(1, d_lru),
        lru_lambda.reshape(1, d_lru), g_v.reshape(1, d_sgu), w_spatial, b_spatial.T, ts)
    x1, h2, sc = _route_call(
        y, x, mod3, w_out.astype(_BF16), g_norm_ffn.reshape(1, d), w_query.astype(_BF16), sub_keys.astype(_BF16), tq)
    t_total = bsz * seq
    nsel = sub_keys.shape[0] * PEER_TOPK
    tab = _pack_call(expert_u, expert_v, min(512, expert_u.shape[0]))
    out = _peer_call(sc, h2.reshape(t_total, d), x1.reshape(t_total, d), mod3, g_out.reshape(1, d), tab, tb, seq, nsel)
    return out.reshape(bsz, seq, d)


def kernel(x, c, w_ada, b_ada, g_norm_mix, w_in, conv_w, conv_b, w_gate_a, b_gate_a, w_gate_i, b_gate_i,
           lru_lambda, g_v, w_spatial, b_spatial, w_out, g_norm_ffn, w_query, sub_keys, expert_u, expert_v, g_final):
    depth = w_ada.shape[0]
    assert depth == 1, "the final RMSNorm is fused into the layer's last call"
    bsz, seq, d = x.shape
    ts = min(512, seq)
    tq = min(512, seq)
    tb = 16
    n_mod = w_ada.shape[-1]
    mod = _ada_call(c, w_ada[0], b_ada[0], 1024 if n_mod % 1024 == 0 else n_mod)
    mod3 = mod.reshape(bsz, 6, d)
    return _layer(x, mod3, g_norm_mix[0], w_in[0], conv_w[0], conv_b[0], w_gate_a[0], b_gate_a[0], w_gate_i[0],
                  b_gate_i[0], lru_lambda[0], g_v[0], w_spatial[0], b_spatial[0], w_out[0], g_norm_ffn[0],
                  w_query[0], sub_keys[0], expert_u[0], expert_v[0], g_final, ts, tq, tb)
```

```python
import functools

import jax
import jax.numpy as jnp
from jax import lax
from jax.experimental import pallas as pl
from jax.experimental.pallas import tpu as pltpu

_F32 = jnp.float32
_BF16 = jnp.bfloat16
_I32 = jnp.int32

EPS = 1e-6
LRU_C = 8.0
CONV_WIDTH = 4
PEER_TOPK = 16
LANES = 128
SUBLANES = 8
_NEG_INF = float("-inf")
_POS_INF = float("inf")
_MIB = 1024 * 1024
VMEM_BYTES = 64 * _MIB


def _vmem_limit(resident_bytes):
    return int(min(resident_bytes + 24 * _MIB, VMEM_BYTES * 15 // 16))


def _gelu(x):
    return 0.5 * x * (1.0 + jnp.tanh(0.7978845608028654 * (x + 0.044715 * (x * x * x))))


def _softplus(y):
    return jnp.maximum(y, 0.0) + jnp.log1p(jnp.exp(-jnp.abs(y)))


def _rms(x, g):
    return x * lax.rsqrt(jnp.mean(x * x, axis=-1, keepdims=True) + EPS) * g


def _bdot(a, b):
    return jnp.dot(a, b, preferred_element_type=_F32)


def _ada_kernel(c_ref, w_ref, b_ref, o_ref):
    c = c_ref[...]
    a = c * jax.nn.sigmoid(c)
    o_ref[...] = _bdot(a.astype(_BF16), w_ref[...].astype(_BF16)) + b_ref[...]


def _ada_call(c, w, b, tn):
    bsz, d = c.shape
    n = w.shape[1]
    return pl.pallas_call(
        _ada_kernel,
        grid=(n // tn,),
        in_specs=[
            pl.BlockSpec((bsz, d), lambda j: (0, 0)),
            pl.BlockSpec((d, tn), lambda j: (0, j)),
            pl.BlockSpec((1, tn), lambda j: (0, j)),
        ],
        out_specs=pl.BlockSpec((bsz, tn), lambda j: (0, j)),
        out_shape=jax.ShapeDtypeStruct((bsz, n), _F32),
        compiler_params=pltpu.CompilerParams(
            dimension_semantics=("arbitrary",), vmem_limit_bytes=_vmem_limit(2 * d * tn * 4)),
    )(c, w, b.reshape(1, n))


def _mixer_kernel(x_ref, mod_ref, gmix_ref, win_ref, cw_ref, cb_ref, wa_ref, ba_ref, wi_ref, bi_ref,
                  lam_ref, gv_ref, wsp_ref, bspt_ref, o_ref, ext_ref, hprev_ref, *, ts, d_lru, heads, groups, chunk):
    s = pl.program_id(1)

    @pl.when(s == 0)
    def _():
        ext_ref[0:SUBLANES, :] = jnp.zeros((SUBLANES, d_lru), _F32)
        hprev_ref[...] = jnp.zeros_like(hprev_ref)

    x = x_ref[...]
    shift = mod_ref[0:1, :]
    scale = mod_ref[1:2, :]
    h = _rms(x, gmix_ref[...]) * (1.0 + scale) + shift
    hb = h.astype(_BF16)

    x_lru = _bdot(hb, win_ref[:, 0:d_lru])
    ext_ref[SUBLANES:SUBLANES + ts, :] = x_lru
    xc = cb_ref[...] + cw_ref[3:4, :] * x_lru
    for k in range(CONV_WIDTH - 1):
        off = SUBLANES - (CONV_WIDTH - 1) + k
        xc = xc + cw_ref[k:k + 1, :] * ext_ref[off:off + ts, :]
    ext_ref[0:SUBLANES, :] = x_lru[ts - SUBLANES:ts, :]

    xcb = xc.astype(_BF16)
    blk = d_lru // heads
    ra, ia = [], []
    for hh in range(heads):
        xh = xcb[:, hh * blk:(hh + 1) * blk]
        ra.append(_bdot(xh, wa_ref[hh]))
        ia.append(_bdot(xh, wi_ref[hh]))
    r = jax.nn.sigmoid(jnp.concatenate(ra, axis=1) + ba_ref[...])
    ig = jax.nn.sigmoid(jnp.concatenate(ia, axis=1) + bi_ref[...])
    log_a = (-LRU_C) * r * _softplus(-lam_ref[...])
    a = jnp.exp(log_a)
    bv = jnp.sqrt(-jnp.tanh(log_a) * (a * a + 1.0)) * (ig * xc)

    sub = lax.broadcasted_iota(_I32, (ts, d_lru), 0) % SUBLANES
    dist = 1
    while dist < SUBLANES:
        a_s = pltpu.roll(a, dist, 0)
        b_s = pltpu.roll(bv, dist, 0)
        m = sub >= dist
        bv = jnp.where(m, a * b_s + bv, bv)
        a = jnp.where(m, a * a_s, a)
        dist *= 2
    carry = hprev_ref[0:1, :]
    groups_h = []
    for gi in range(ts // SUBLANES):
        hg = bv[gi * SUBLANES:(gi + 1) * SUBLANES, :] + a[gi * SUBLANES:(gi + 1) * SUBLANES, :] * carry
        groups_h.append(hg)
        carry = hg[SUBLANES - 1:SUBLANES, :]
    hs = jnp.concatenate(groups_h, axis=0)
    hprev_ref[0:1, :] = carry

    y_gate = _bdot(hb, win_ref[:, d_lru:2 * d_lru])
    o_ref[:, 0:d_lru] = (hs * _gelu(y_gate)).astype(_BF16)

    d_sgu = gv_ref.shape[1]
    gd = d_sgu // groups
    u = _gelu(_bdot(hb, win_ref[:, 2 * d_lru:2 * d_lru + d_sgu]))
    v = _gelu(_bdot(hb, win_ref[:, 2 * d_lru + d_sgu:2 * d_lru + 2 * d_sgu]))
    vnb = _rms(v, gv_ref[...]).astype(_BF16)
    tri = lax.broadcasted_iota(_I32, (chunk, chunk), 0) >= lax.broadcasted_iota(_I32, (chunk, chunk), 1)
    for g in range(groups):
        wm = jnp.where(tri, wsp_ref[g], 0.0).astype(_BF16)
        bcol = bspt_ref[:, g:g + 1]
        for n in range(ts // chunk):
            sg = _bdot(wm, vnb[n * chunk:(n + 1) * chunk, g * gd:(g + 1) * gd]) + bcol
            ug = u[n * chunk:(n + 1) * chunk, g * gd:(g + 1) * gd]
            o_ref[n * chunk:(n + 1) * chunk, d_lru + g * gd:d_lru + (g + 1) * gd] = (ug * sg).astype(_BF16)


def _mixer_call(x, mod3, g_mix, w_in_b, conv_w, conv_b, w_a_b, b_a, w_i_b, b_i, lam, g_v, w_sp, b_sp_t, ts):
    bsz, seq, d = x.shape
    d_lru = conv_w.shape[1]
    d_sgu = g_v.shape[1]
    heads = w_a_b.shape[0]
    groups, chunk, _ = w_sp.shape
    d_mix = d_lru + d_sgu
    const = lambda shape: pl.BlockSpec(shape, lambda b, s: (0,) * len(shape), pipeline_mode=pl.Buffered(1))
    kern = functools.partial(_mixer_kernel, ts=ts, d_lru=d_lru, heads=heads, groups=groups, chunk=chunk)
    return pl.pallas_call(
        kern,
        grid=(bsz, seq // ts),
        in_specs=[
            pl.BlockSpec((None, ts, d), lambda b, s: (b, s, 0)),
            pl.BlockSpec((None, 6, d), lambda b, s: (b, 0, 0)),
            const((1, d)),
            const(w_in_b.shape),
            const(conv_w.shape),
            const((1, d_lru)),
            const(w_a_b.shape),
            const((1, d_lru)),
            const(w_i_b.shape),
            const((1, d_lru)),
            const((1, d_lru)),
            const((1, d_sgu)),
            const(w_sp.shape),
            const(b_sp_t.shape),
        ],
        out_specs=pl.BlockSpec((None, ts, d_mix), lambda b, s: (b, s, 0)),
        out_shape=jax.ShapeDtypeStruct((bsz, seq, d_mix), _BF16),
        scratch_shapes=[pltpu.VMEM((ts + SUBLANES, d_lru), _F32), pltpu.VMEM((SUBLANES, d_lru), _F32)],
        compiler_params=pltpu.CompilerParams(
            dimension_semantics=("arbitrary", "arbitrary"),
            vmem_limit_bytes=_vmem_limit(w_in_b.size * 2 + 2 * ts * d * 4 + 2 * ts * d_mix * 2 + 12 * ts * d_lru * 4)),
    )(x, mod3, g_mix, w_in_b, conv_w, conv_b, w_a_b, b_a, w_i_b, b_i, lam, g_v, w_sp, b_sp_t)


def _topk_cols(vals, pos, ids, k, tick=None):
    n, t = vals.shape
    kio = lax.broadcasted_iota(_I32, (k, t), 0)
    out_v = jnp.zeros((k, t), _F32)
    out_i = jnp.zeros((k, t), _F32)
    for j in range(k):
        if tick is not None:
            tick()
        m = jnp.max(vals, axis=0, keepdims=True)
        p = jnp.min(jnp.where(vals == m, pos, _POS_INF), axis=0, keepdims=True)
        hit = pos == p
        e = p if ids is None else jnp.max(jnp.where(hit, ids, -1.0), axis=0, keepdims=True)
        out_v = jnp.where(kio == j, m, out_v)
        out_i = jnp.where(kio == j, e, out_i)
        vals = jnp.where(hit, _NEG_INF, vals)
    return out_v, out_i


def _pair_candidates(v1, i1, v2, i2, n_keys):
    k, t = v1.shape
    half = k // 2
    jrow = lax.broadcasted_iota(_I32, (half, t), 0)
    vals = [v1[0:1, :] + v2]
    pos = [lax.broadcasted_iota(_I32, (k, t), 0).astype(_F32)]
    ids = [i1[0:1, :] * n_keys + i2]
    for i in range(1, half):
        keep = jrow < k // (i + 1)
        vals.append(jnp.where(keep, v1[i:i + 1, :] + v2[0:half, :], _NEG_INF))
        pos.append((jrow + i * k).astype(_F32))
        ids.append(i1[i:i + 1, :] * n_keys + i2[0:half, :])
    vals.append(v1[half:k, :] + v2[0:1, :])
    pos.append(((jrow + half) * k).astype(_F32))
    ids.append(i1[half:k, :] * n_keys + i2[0:1, :])
    return jnp.concatenate(vals, axis=0), jnp.concatenate(pos, axis=0), jnp.concatenate(ids, axis=0)


def _route_kernel(y_ref, x_ref, mod_ref, wout_ref, gffn_ref, wq_ref, keys_ref,
                  x1_ref, h2_ref, sc_ref, *, heads):
    x1 = x_ref[...] + mod_ref[2:3, :] * _bdot(y_ref[...], wout_ref[...])
    x1_ref[...] = x1
    h2 = _rms(x1, gffn_ref[...]) * (1.0 + mod_ref[4:5, :]) + mod_ref[3:4, :]
    h2_ref[...] = h2
    q = _bdot(h2.astype(_BF16), wq_ref[...]).astype(_BF16)
    dk = keys_ref.shape[-1]
    nt = (((1,), (1,)), ((), ()))
    for c in range(2 * heads):
        sc_ref[c] = lax.dot_general(keys_ref[c // 2, c % 2], q[:, c * dk:(c + 1) * dk], nt,
                                    preferred_element_type=_F32)


def _route_call(y, x, mod3, w_out_b, g_ffn, w_q_b, keys_b, tq):
    bsz, seq, d = x.shape
    d_mix = y.shape[-1]
    heads, _, n_keys, dk = keys_b.shape
    nq = seq // tq
    const = lambda shape: pl.BlockSpec(shape, lambda b, s: (0,) * len(shape), pipeline_mode=pl.Buffered(1))
    kern = functools.partial(_route_kernel, heads=heads)
    return pl.pallas_call(
        kern,
        grid=(bsz, nq),
        in_specs=[
            pl.BlockSpec((None, tq, d_mix), lambda b, s: (b, s, 0)),
            pl.BlockSpec((None, tq, d), lambda b, s: (b, s, 0)),
            pl.BlockSpec((None, 6, d), lambda b, s: (b, 0, 0)),
            const(w_out_b.shape),
            const((1, d)),
            const(w_q_b.shape),
            const(keys_b.shape),
        ],
        out_specs=[
            pl.BlockSpec((None, tq, d), lambda b, s: (b, s, 0)),
            pl.BlockSpec((None, tq, d), lambda b, s: (b, s, 0)),
            pl.BlockSpec((2 * heads, n_keys, tq), lambda b, s: (0, 0, b * nq + s)),
        ],
        out_shape=[
            jax.ShapeDtypeStruct((bsz, seq, d), _F32),
            jax.ShapeDtypeStruct((bsz, seq, d), _F32),
            jax.ShapeDtypeStruct((2 * heads, n_keys, bsz * seq), _F32),
        ],
        compiler_params=pltpu.CompilerParams(
            dimension_semantics=("arbitrary", "arbitrary"),
            vmem_limit_bytes=_vmem_limit((w_out_b.size + w_q_b.size) * 2 + 2 * tq * (d_mix * 2 + 3 * d * 4)
                                         + 2 * 2 * heads * n_keys * tq * 4)),
    )(y, x, mod3, w_out_b, g_ffn, w_q_b, keys_b)


def _pack_kernel(u_ref, v_ref, o_ref):
    w = pltpu.pack_elementwise([u_ref[...], v_ref[...]], packed_dtype=_BF16)
    for c in range(o_ref.shape[1]):
        o_ref[:, c, :] = w[:, c * LANES:(c + 1) * LANES]


def _pack_call(expert_u, expert_v, rows):
    n, d = expert_u.shape
    spec = pl.BlockSpec((rows, d), lambda i: (i, 0), pipeline_mode=pl.Buffered(3))
    ospec = pl.BlockSpec((rows, d // LANES, LANES), lambda i: (i, 0, 0))

    def outer(u_hbm, v_hbm, o_hbm):
        pltpu.emit_pipeline(_pack_kernel, grid=(n // rows,), in_specs=[spec, spec], out_specs=[ospec])(
            u_hbm, v_hbm, o_hbm)

    any_spec = pl.BlockSpec(memory_space=pl.ANY)
    return pl.pallas_call(
        outer,
        in_specs=[any_spec, any_spec],
        out_specs=any_spec,
        out_shape=jax.ShapeDtypeStruct((n, d // LANES, LANES), jnp.uint32),
        compiler_params=pltpu.CompilerParams(vmem_limit_bytes=_vmem_limit(8 * rows * d * 4)),
    )(expert_u, expert_v)


def _peer_kernel(sc_ref, h_ref, x1_ref, mod_ref, gfin_ref, tab_ref, tabw_ref, o_ref,
                 buf0_ref, buf1_ref, sem_ref, zero_ref, idxv_ref, gt_ref, gs_ref, idxs_ref, ssem_ref,
                 *, tb, nsel, d, nsb, n_keys):
    s = pl.program_id(0)
    k = pl.program_id(1)
    steps = pl.num_programs(1)
    nct = d // LANES
    tok_tiles = (nsel // SUBLANES) * nct
    bufs = (buf0_ref, buf1_ref)
    sb_tokens = sc_ref.shape[-1]
    nblocks = sb_tokens // tb

    def row_copy(row, sl, t, e):
        first = t * tok_tiles + (e // SUBLANES) * nct
        return pltpu.make_async_copy(
            tab_ref.at[row], bufs[sl].at[pl.ds(first, nct), e % SUBLANES, :], sem_ref.at[sl, t])

    def token_wait(sl, t):
        pltpu.make_async_copy(
            tabw_ref.at[pl.ds(0, tok_tiles)], bufs[sl].at[pl.ds(t * tok_tiles, tok_tiles)], sem_ref.at[sl, t]).wait()

    key_pos = lax.broadcasted_iota(_I32, (n_keys, sb_tokens), 0).astype(_F32)

    def route_head(hl, tick):
        tops = [_topk_cols(sc_ref[2 * hl + p], key_pos, None, PEER_TOPK, tick) for p in range(2)]
        (v1, i1), (v2, i2) = tops
        cand, cpos, cids = _pair_candidates(v1, i1, v2, i2, float(n_keys))
        tv, ti = _topk_cols(cand, cpos, cids, PEER_TOPK, tick)
        ex = jnp.exp(tv - tv[0:1, :])
        r = pl.multiple_of((2 * k + hl) * PEER_TOPK, PEER_TOPK)
        idxv_ref[pl.ds(r, PEER_TOPK), :] = ti.astype(_I32)
        gt_ref[pl.ds(r, PEER_TOPK), :] = ex / jnp.sum(ex, axis=0, keepdims=True)

    def ids_to_smem(slot):
        return pltpu.make_async_copy(idxv_ref, idxs_ref.at[slot], ssem_ref)

    def publish():
        @pl.when(k == steps - 1)
        def _():
            gs_ref[s % 3] = gt_ref[...].T
            ids_to_smem(s % 3).start()

    def published():
        @pl.when(jnp.logical_and(k == 0, s > 0))
        def _():
            ids_to_smem((s - 1) % 3).wait()

    ngrp = nsel // SUBLANES
    rows8 = lax.broadcasted_iota(_I32, (SUBLANES, 2 * nsel), 0)
    lane = lax.broadcasted_iota(_I32, (nsel, LANES), 1)
    spread = (lax.broadcasted_iota(_I32, (nsel, 2 * nsel), 1)
              == 2 * lax.broadcasted_iota(_I32, (nsel, 2 * nsel), 0) + 1).astype(_BF16)
    final = jnp.logical_and(s == nsb + 1, k == steps - 1)

    def token_matrix(sl, t):
        w = bufs[sl][t * tok_tiles:(t + 1) * tok_tiles]
        rows = [jnp.concatenate([w[g * nct + c] for c in range(nct)], axis=1) for g in range(ngrp)]
        return pltpu.bitcast(jnp.concatenate(rows, axis=0), _BF16)

    def u_dots(sl, t):
        acc = [jnp.zeros((SUBLANES, LANES), _F32) for _ in range(ngrp)]
        for c in range(nct):
            xc = h_ref[sl * tb + t:sl * tb + t + 1, c * LANES:(c + 1) * LANES] + zero_ref[...]
            for g in range(ngrp):
                w = bufs[sl][t * tok_tiles + g * nct + c]
                u = pltpu.unpack_elementwise(w, index=0, packed_dtype=_BF16, unpacked_dtype=_F32)
                acc[g] = acc[g] + u * xc
        return jnp.sum(jnp.concatenate(acc, axis=0), axis=1, keepdims=True)

    def evaluate(sl):
        nsl = 1 - sl
        blk = 2 * k + sl
        nxt = jnp.where(final, blk, blk + 1) if sl == 1 else blk + 1
        nslot = (s - 2 + nxt // nblocks) % 3
        ntok0 = (nxt % nblocks) * tb
        row0 = sl * tb

        def next_row(t, e):
            return idxs_ref[nslot, e, ntok0 + t]
        ngroups = tb // SUBLANES
        copies = [(t, e) for t in range(tb) for e in range(nsel)]
        nticks = 3 * PEER_TOPK
        per_tick = len(copies) // (6 * nticks)
        issued = [0]

        def issue(n):
            for t, e in copies[issued[0]:issued[0] + n]:
                row_copy(next_row(t, e), nsl, t, e).start(priority=e % 2)
            issued[0] += n

        token_wait(sl, 0)
        if sl == 0:
            published()
        route_head(sl, lambda: issue(per_tick))
        if sl == 1:
            publish()
        nregions = (ngroups + 1) * SUBLANES
        share = -(-(len(copies) - issued[0]) // nregions)
        gslot = (s - 2) % 3
        act2 = None
        for p in range(ngroups + 1):
            r0, v0 = p * SUBLANES, (p - 1) * SUBLANES
            zcols = jnp.zeros((nsel, LANES), _F32)
            out = jnp.zeros((SUBLANES, d), _F32)
            for j in range(SUBLANES):
                if p < ngroups and r0 + j > 0:
                    token_wait(sl, r0 + j)
                issue(share)
                if p > 0:
                    aj = jnp.where(rows8 == j, act2, 0.0).astype(_BF16)
                    out = out + _bdot(aj, token_matrix(sl, v0 + j))
                if p < ngroups:
                    zcols = jnp.where(lane == j, u_dots(sl, r0 + j), zcols)
            if p > 0:
                x2 = x1_ref[row0 + v0:row0 + v0 + SUBLANES, :] + mod_ref[5:6, :] * out
                o_ref[row0 + v0:row0 + v0 + SUBLANES, :] = _rms(x2, gfin_ref[...])
            if p < ngroups:
                z = zcols.T[0:SUBLANES, :]
                g_rows = gs_ref[gslot, pl.ds(pl.multiple_of(blk * tb + r0, SUBLANES), SUBLANES), :]
                act = (_gelu(z) * g_rows).astype(_BF16)
                act2 = _bdot(act, spread)

    @pl.when(s < 2)
    def _():
        @pl.when(jnp.logical_and(s == 0, k == 0))
        def _():
            zero_ref[...] = jnp.zeros_like(zero_ref)

        published()
        route_head(0, None)
        route_head(1, None)
        publish()

        @pl.when(jnp.logical_and(s == 1, k == steps - 1))
        def _():
            def body(t, c):
                for e in range(nsel):
                    row_copy(idxs_ref[0, e, t], 0, t, e).start(priority=e % 2)
                return c
            lax.fori_loop(0, tb, body, 0)

    @pl.when(s >= 2)
    def _():
        evaluate(0)
        evaluate(1)

    @pl.when(final)
    def _():
        ids_to_smem(s % 3).wait()
        for t in range(tb):
            token_wait(0, t)


def _peer_call(sc, h2, x1, mod3, g_final, tab3, tb, seq, nsel):
    nhh, n_keys, t_total = sc.shape
    d = h2.shape[-1]
    sb_tokens = LANES
    nsteps = sb_tokens // (2 * tb)
    assert nhh == 4 * nsteps, "two routing heads per evaluation step"
    nsb = t_total // sb_tokens
    nct = d // LANES
    n_exp = tab3.shape[0]
    tabw = tab3.reshape(n_exp * nct // SUBLANES, SUBLANES, LANES)
    slot_tiles = tb * (nsel // SUBLANES) * nct
    blk = lambda s, k: jnp.maximum((s - 2) * nsteps + k, 0)
    kern = functools.partial(_peer_kernel, tb=tb, nsel=nsel, d=d, nsb=nsb, n_keys=n_keys)
    return pl.pallas_call(
        kern,
        grid=(nsb + 2, nsteps),
        in_specs=[
            pl.BlockSpec((4, n_keys, sb_tokens), lambda s, k: (k, 0, jnp.minimum(s, nsb - 1))),
            pl.BlockSpec((2 * tb, d), lambda s, k: (blk(s, k), 0)),
            pl.BlockSpec((2 * tb, d), lambda s, k: (blk(s, k), 0)),
            pl.BlockSpec((None, 6, d), lambda s, k: ((blk(s, k) * 2 * tb) // seq, 0, 0)),
            pl.BlockSpec((1, d), lambda s, k: (0, 0)),
            pl.BlockSpec(memory_space=pl.ANY),
            pl.BlockSpec(memory_space=pl.ANY),
        ],
        out_specs=pl.BlockSpec((2 * tb, d), lambda s, k: (blk(s, k), 0)),
        out_shape=jax.ShapeDtypeStruct((t_total, d), _F32),
        scratch_shapes=[
            pltpu.VMEM((slot_tiles, SUBLANES, LANES), jnp.uint32),
            pltpu.VMEM((slot_tiles, SUBLANES, LANES), jnp.uint32),
            pltpu.SemaphoreType.DMA((2, tb)),
            pltpu.VMEM((SUBLANES, LANES), _F32),
            pltpu.VMEM((nsel, sb_tokens), _I32),
            pltpu.VMEM((nsel, sb_tokens), _F32),
            pltpu.VMEM((3, sb_tokens, nsel), _F32),
            pltpu.SMEM((3, nsel, sb_tokens), _I32),
            pltpu.SemaphoreType.DMA(()),
        ],
        compiler_params=pltpu.CompilerParams(
            dimension_semantics=("arbitrary", "arbitrary"),
            vmem_limit_bytes=_vmem_limit(2 * slot_tiles * SUBLANES * LANES * 4 + 2 * 3 * 2 * tb * d * 4)),
    )(sc, h2, x1, mod3, g_final, tab3, tabw)


def _layer(x, mod3, g_norm_mix, w_in, conv_w, conv_b, w_gate_a, b_gate_a, w_gate_i, b_gate_i, lru_lambda, g_v,
           w_spatial, b_spatial, w_out, g_norm_ffn, w_query, sub_keys, expert_u, expert_v, g_out, ts, tq, tb):
    bsz, seq, d = x.shape
    d_lru = conv_w.shape[-1]
    d_sgu = g_v.shape[-1]
    y = _mixer_call(
        x, mod3, g_norm_mix.reshape(1, d), w_in.astype(_BF16), conv_w, conv_b.reshape(1, d_lru),
        w_gate_a.astype(_BF16), b_gate_a.reshape(1, d_lru), w_gate_i.astype(_BF16), b_gate_i.reshape(1, d_lru),
        lru_lambda.reshape(1, d_lru), g_v.reshape(1, d_sgu), w_spatial, b_spatial.T, ts)
    x1, h2, sc = _route_call(
        y, x, mod3, w_out.astype(_BF16), g_norm_ffn.reshape(1, d), w_query.astype(_BF16), sub_keys.astype(_BF16), tq)
    t_total = bsz * seq
    nsel = sub_keys.shape[0] * PEER_TOPK
    tab = _pack_call(expert_u, expert_v, min(512, expert_u.shape[0]))
    out = _peer_call(sc, h2.reshape(t_total, d), x1.reshape(t_total, d), mod3, g_out.reshape(1, d), tab, tb, seq, nsel)
    return out.reshape(bsz, seq, d)


def kernel(x, c, w_ada, b_ada, g_norm_mix, w_in, conv_w, conv_b, w_gate_a, b_gate_a, w_gate_i, b_gate_i,
           lru_lambda, g_v, w_spatial, b_spatial, w_out, g_norm_ffn, w_query, sub_keys, expert_u, expert_v, g_final):
    depth = w_ada.shape[0]
    assert depth == 1, "the final RMSNorm is fused into the layer's last call"
    bsz, seq, d = x.shape
    ts = min(256, seq)
    tq = min(512, seq)
    tb = 16
    n_mod = w_ada.shape[-1]
    mod = _ada_call(c, w_ada[0], b_ada[0], 1024 if n_mod % 1024 == 0 else n_mod)
    mod3 = mod.reshape(bsz, 6, d)
    return _layer(x, mod3, g_norm_mix[0], w_in[0], conv_w[0], conv_b[0], w_gate_a[0], b_gate_a[0], w_gate_i[0],
                  b_gate_i[0], lru_lambda[0], g_v[0], w_spatial[0], b_spatial[0], w_out[0], g_norm_ffn[0],
                  w_query[0], sub_keys[0], expert_u[0], expert_v[0], g_final, ts, tq, tb)
```
